```python
import math, functools
import jax, jax.numpy as jnp
from jax import lax
import numpy as np

D_MODEL = 1024
BATCH = 2
SEQ = 8192
DEPTH = 4
DEC_BATCH = 128
DEC_SEQ = 4
PAST_LEN = 2048
PAGE_SIZE = 128

MIX_W = D_MODEL // 2
N_BRANCH = 3
CONV_W = 4
GDN_DK = 128
GDN_DV = 128
GDN_HEADS = MIX_W // GDN_DV
GDN_CHUNK = 64
FOX_DH = 64
FOX_HEADS = MIX_W // FOX_DH
FOX_QBLK = 128
FOX_SCALE = FOX_DH ** -0.5
SSM_P = 64
SSM_HEADS = MIX_W // SSM_P
SSM_GROUPS = 2
SSM_N = 64
SSM_CHUNK = 64
PEER_HEADS = 8
PEER_NKEYS = 128
PEER_NEXP = PEER_NKEYS * PEER_NKEYS
PEER_DQ = 256
PEER_TOPK = 16
PEER_TBLK = 256
EPS = 1e-6

GDN_CONV_CH = 2 * GDN_HEADS * GDN_DK + GDN_HEADS * GDN_DV
SSM_CONV_CH = MIX_W + 2 * SSM_GROUPS * SSM_N
IN_SPLITS = (GDN_CONV_CH, MIX_W, GDN_HEADS, GDN_HEADS,
             3 * MIX_W, FOX_HEADS,
             MIX_W, SSM_CONV_CH, SSM_HEADS,
             N_BRANCH * D_MODEL)
N_IN = sum(IN_SPLITS)

kernel_name = 'hybrid_gdn_fox_ssd_peer_adaln_step'


def _split(t, sizes):
    return jnp.split(t, np.cumsum(sizes)[:-1].tolist(), axis=-1)


def _rmsnorm(x, w):
    xf = x.astype(jnp.float32)
    y = xf * lax.rsqrt(jnp.mean(xf * xf, axis=-1, keepdims=True) + EPS)
    return (y * w).astype(x.dtype)


def _l2norm(x):
    xf = x.astype(jnp.float32)
    return xf * lax.rsqrt(jnp.sum(xf * xf, axis=-1, keepdims=True) + EPS)


def _causal_conv(x, buf, w):
    L = x.shape[1]
    xp = jnp.concatenate([buf.astype(x.dtype), x], axis=1)
    y = sum(xp[:, i:i + L] * w[i] for i in range(CONV_W))
    return y, xp[:, L:]


def _to_chunks(t, C):
    B, L = t.shape[:2]
    n = -(-L // C)
    t = jnp.pad(t, [(0, 0), (0, n * C - L)] + [(0, 0)] * (t.ndim - 2))
    t = t.reshape((B, n, C) + t.shape[2:])
    perm = (1, 0, 3, 2) + tuple(range(4, t.ndim))
    return jnp.transpose(t, perm).astype(jnp.float32)


def _from_chunks(o, L):
    n, B, H, C, d = o.shape
    return jnp.transpose(o, (1, 0, 3, 2, 4)).reshape(B, n * C, H, d)[:, :L]


def _gdn_chunked(q, k, v, g, beta, s0):
    L = q.shape[1]
    C = min(GDN_CHUNK, L)
    qc, kc, vc = _to_chunks(q, C), _to_chunks(k, C), _to_chunks(v, C)
    gc = jnp.cumsum(_to_chunks(g, C), axis=-1)
    bc = _to_chunks(beta, C)
    tril = jnp.tril(jnp.ones((C, C), dtype=bool))
    strict = jnp.tril(jnp.ones((C, C), dtype=bool), -1)
    decay = jnp.where(tril, jnp.exp(jnp.where(tril, gc[..., :, None] - gc[..., None, :], 0.0)), 0.0)
    kb = kc * bc[..., None]
    m = jnp.where(strict, jnp.einsum('...id,...jd->...ij', kb, kc) * decay, 0.0)
    eye = jnp.eye(C, dtype=jnp.float32)
    t_inv = lax.linalg.triangular_solve(eye + m, jnp.broadcast_to(eye, m.shape),
                                        left_side=True, lower=True, unit_diagonal=True)
    u_in = t_inv @ (vc * bc[..., None])
    wk_in = t_inv @ (kb * jnp.exp(gc)[..., None])
    qk = jnp.einsum('...id,...jd->...ij', qc, kc) * decay
    q_dec = qc * jnp.exp(gc)[..., None]
    k_dec = kc * jnp.exp(gc[..., -1:] - gc)[..., None]
    g_last = jnp.exp(gc[..., -1])

    def step(s, inp):
        u_i, w_i, qk_i, qd_i, kd_i, gl_i = inp
        u = u_i - w_i @ s
        o = qd_i @ s + qk_i @ u
        s = s * gl_i[..., None, None] + jnp.swapaxes(kd_i, -1, -2) @ u
        return s, o

    s, o = lax.scan(step, s0.astype(jnp.float32), (u_in, wk_in, qk, q_dec, k_dec, g_last))
    return _from_chunks(o, L), s


def _ssd_chunked(x, dt, A, bm, cm, h0):
    L, H = x.shape[1], x.shape[2]
    rep = H // SSM_GROUPS
    C = min(SSM_CHUNK, L)
    xc = _to_chunks(x, C)
    bc = _to_chunks(jnp.repeat(bm, rep, axis=2), C)
    cc = _to_chunks(jnp.repeat(cm, rep, axis=2), C)
    dtc = _to_chunks(dt, C)
    a = jnp.cumsum(dtc * A[:, None], axis=-1)
    tril = jnp.tril(jnp.ones((C, C), dtype=bool))
    seg = jnp.where(tril, jnp.exp(jnp.where(tril, a[..., :, None] - a[..., None, :], 0.0)), 0.0)
    xdt = xc * dtc[..., None]
    y_intra = (jnp.einsum('...in,...jn->...ij', cc, bc) * seg) @ xdt
    c_dec = cc * jnp.exp(a)[..., None]
    b_dec = bc * jnp.exp(a[..., -1:] - a)[..., None]
    a_last = jnp.exp(a[..., -1])

    def step(h, inp):
        y_i, c_i, b_i, x_i, al_i = inp
        y = y_i + jnp.einsum('bhcn,bhpn->bhcp', c_i, h)
        h = h * al_i[..., None, None] + jnp.einsum('bhcp,bhcn->bhpn', x_i, b_i)
        return h, y

    h, y = lax.scan(step, h0.astype(jnp.float32), (y_intra, c_dec, b_dec, xdt, a_last))
    return _from_chunks(y, L), h


def _fox_prompt(q, k, v, logf):
    B, S, H, Dh = q.shape
    nb = S // FOX_QBLK
    cum_k = jnp.swapaxes(jnp.cumsum(logf, axis=1), 1, 2)
    q_blk = jnp.swapaxes(q.reshape(B, nb, FOX_QBLK, H, Dh), 0, 1)
    c_blk = jnp.transpose(cum_k.reshape(B, H, nb, FOX_QBLK), (2, 0, 1, 3))
    key_pos = jnp.arange(S)

    def block(inp):
        i, qi, ci = inp
        s = jnp.einsum('bqhd,bkhd->bhqk', qi, k).astype(jnp.float32) * FOX_SCALE
        s = s + ci[..., :, None] - cum_k[..., None, :]
        q_pos = i * FOX_QBLK + jnp.arange(FOX_QBLK)
        s = jnp.where(key_pos[None, :] <= q_pos[:, None], s, -jnp.inf)
        p = jax.nn.softmax(s, axis=-1).astype(v.dtype)
        return jnp.einsum('bhqk,bkhd->bqhd', p, v)

    o = lax.map(block, (jnp.arange(nb), q_blk, c_blk))
    return jnp.swapaxes(o, 0, 1).reshape(B, S, H, Dh)


def _fox_sample(q, k, v, logf, k_past, v_past, logf_past):
    P, L = k_past.shape[1], q.shape[1]
    cum_past = jnp.cumsum(logf_past.astype(jnp.float32), axis=1)
    cum_new = cum_past[:, -1:] + jnp.cumsum(logf, axis=1)
    keys = jnp.concatenate([k_past.astype(k.dtype), k], axis=1)
    vals = jnp.concatenate([v_past.astype(v.dtype), v], axis=1)
    cum_k = jnp.concatenate([cum_past, cum_new], axis=1)
    s = jnp.einsum('bqhd,bkhd->bhqk', q, keys).astype(jnp.float32) * FOX_SCALE
    s = s + jnp.swapaxes(cum_new, 1, 2)[..., :, None] - jnp.swapaxes(cum_k, 1, 2)[..., None, :]
    allowed = jnp.arange(P + L)[None, :] <= (P + jnp.arange(L))[:, None]
    p = jax.nn.softmax(jnp.where(allowed, s, -jnp.inf), axis=-1).astype(vals.dtype)
    return jnp.einsum('bhqk,bkhd->bqhd', p, vals)


def _peer(h, wq, sub_keys, u_tab, v_tab):
    B, L, D = h.shape
    T = B * L
    blk = min(PEER_TBLK, T)
    nblk = -(-T // blk)
    ht = jnp.pad(h.reshape(T, D), ((0, nblk * blk - T), (0, 0))).reshape(nblk, blk, D)

    def block(hb):
        q = (hb @ wq).reshape(blk, PEER_HEADS, 2, PEER_DQ // 2)
        s = jnp.einsum('thpk,pnk->thpn', q, sub_keys).astype(jnp.float32)
        sv, si = lax.top_k(s, PEER_TOPK)
        cand = (sv[:, :, 0, :, None] + sv[:, :, 1, None, :]).reshape(blk, PEER_HEADS, -1)
        cid = (si[:, :, 0, :, None] * PEER_NKEYS + si[:, :, 1, None, :]).reshape(blk, PEER_HEADS, -1)
        top, pos = lax.top_k(cand, PEER_TOPK)
        eid = jnp.take_along_axis(cid, pos, axis=-1)
        gate = jax.nn.softmax(top, axis=-1)
        pre = jnp.einsum('td,thkd->thk', hb, u_tab[eid]).astype(jnp.float32)
        act = jax.nn.gelu(pre, approximate=False)
        return jnp.einsum('thk,thkd->td', (gate * act).astype(hb.dtype), v_tab[eid])

    return lax.map(block, ht).reshape(nblk * blk, D)[:T].reshape(B, L, D)


def _token_mix(h, lw, states, attend):
    B, L, _ = h.shape
    f32 = jnp.float32
    gdn_s, gdn_b, ssm_s, ssm_b = states
    (a_qkv, a_gate, a_beta, a_alpha, b_qkv, b_f, c_z, c_xbc, c_dt,
     gate_logits) = _split(h @ lw['w_in'], IN_SPLITS)
    a_conv, gdn_b_new = _causal_conv(a_qkv, gdn_b, lw['gdn_conv_w'])
    q, k, v = _split(jax.nn.silu(a_conv), (GDN_HEADS * GDN_DK, GDN_HEADS * GDN_DK, GDN_HEADS * GDN_DV))
    q = _l2norm(q.reshape(B, L, GDN_HEADS, GDN_DK)) * GDN_DK ** -0.5
    k = _l2norm(k.reshape(B, L, GDN_HEADS, GDN_DK))
    v = v.reshape(B, L, GDN_HEADS, GDN_DV)
    beta = jax.nn.sigmoid(a_beta.astype(f32))
    g = -jnp.exp(lw['gdn_a_log'].astype(f32)) * jax.nn.softplus(a_alpha.astype(f32) + lw['gdn_dt_bias'])
    o_a, gdn_s_new = _gdn_chunked(q, k, v, g, beta, gdn_s)
    o_a = _rmsnorm(o_a, lw['gdn_norm_w']) * jax.nn.silu(a_gate.reshape(B, L, GDN_HEADS, GDN_DV).astype(f32))
    o_a = o_a.reshape(B, L, MIX_W).astype(h.dtype)
    qf, kf, vf = [t.reshape(B, L, FOX_HEADS, FOX_DH) for t in jnp.split(b_qkv, 3, axis=-1)]
    logf = jax.nn.log_sigmoid(b_f.astype(f32) + lw['fox_f_bias'])
    o_b = attend(qf, kf, vf, logf).reshape(B, L, MIX_W)
    xbc, ssm_b_new = _causal_conv(c_xbc, ssm_b, lw['ssm_conv_w'])
    xbc = jax.nn.silu(xbc + lw['ssm_conv_b'])
    xs, bm, cm = _split(xbc, (MIX_W, SSM_GROUPS * SSM_N, SSM_GROUPS * SSM_N))
    xs = xs.reshape(B, L, SSM_HEADS, SSM_P)
    dt = jax.nn.softplus(c_dt.astype(f32) + lw['ssm_dt_bias'])
    A = -jnp.exp(lw['ssm_a_log'].astype(f32))
    y, ssm_s_new = _ssd_chunked(xs, dt, A, bm.reshape(B, L, SSM_GROUPS, SSM_N),
                                cm.reshape(B, L, SSM_GROUPS, SSM_N), ssm_s)
    y = y + lw['ssm_d'][:, None] * xs.astype(f32)
    o_c = _rmsnorm(y.reshape(B, L, MIX_W) * jax.nn.silu(c_z.astype(f32)), lw['ssm_norm_w']).astype(h.dtype)
    branches = jnp.stack([o_a, o_b, o_c], axis=2)
    gates = jax.nn.sigmoid(gate_logits).reshape(B, L, N_BRANCH, D_MODEL)
    merged = jnp.einsum('blim,imd->blid', branches, lw['w_branch'])
    out = jnp.sum(gates * merged, axis=2) @ lw['w_out']
    return out, (kf, vf, logf, gdn_s_new, gdn_b_new, ssm_s_new, ssm_b_new)


def _layer(x, c, lw, states, attend):
    mod = jax.nn.silu(c) @ lw['w_ada'] + lw['b_ada']
    sh1, sc1, g1, sh2, sc2, g2 = [m[:, None, :] for m in jnp.split(mod, 6, axis=-1)]
    h = _rmsnorm(x, lw['norm_mix']) * (1 + sc1) + sh1
    mix, new_state = _token_mix(h, lw, states, attend)
    x = x + g1 * mix
    h = _rmsnorm(x, lw['norm_ffn']) * (1 + sc2) + sh2
    x = x + g2 * _peer(h, lw['peer_wq'], lw['peer_keys'], lw['peer_u'], lw['peer_v'])
    return x, new_state


def _inv_softplus_dt(k, shape):
    dt = jnp.exp(jax.random.uniform(k, shape, jnp.float32, minval=math.log(1e-3), maxval=math.log(1e-1)))
    return dt + jnp.log(-jnp.expm1(-dt))


def setup_inputs(seed: int = 0) -> dict:
    key = jax.random.key(seed)
    keys = jax.random.split(key, 48)
    counter = iter(range(48))

    def nk():
        return keys[next(counter)]

    def nrm(shape, scale):
        return jax.random.normal(nk(), shape, jnp.float32) * scale

    def gain(shape):
        return 1.0 + nrm(shape, 0.02)

    n_pages = PAST_LEN // PAGE_SIZE
    n_used = DEC_BATCH * n_pages
    n_pool = n_used + max(1, n_used // 4)
    page_table = jax.random.permutation(nk(), n_pool)[:n_used].reshape(DEC_BATCH, n_pages).astype(jnp.int32)
    return {
        'x_prompt': nrm((BATCH, SEQ, D_MODEL), 1.0),
        'x_sample': nrm((DEC_BATCH, DEC_SEQ, D_MODEL), 1.0),
        'cache_k': nrm((DEPTH, n_pool, PAGE_SIZE, FOX_HEADS, FOX_DH), 1.0),
        'cache_v': nrm((DEPTH, n_pool, PAGE_SIZE, FOX_HEADS, FOX_DH), 1.0),
        'cache_logf': jax.nn.log_sigmoid(1.0 + nrm((DEPTH, n_pool, PAGE_SIZE, FOX_HEADS), 1.0)),
        'page_table': page_table,
        'state_gdn': nrm((DEPTH, DEC_BATCH, GDN_HEADS, GDN_DK, GDN_DV), 0.1),
        'state_gdn_conv': nrm((DEPTH, DEC_BATCH, CONV_W - 1, GDN_CONV_CH), 1.0),
        'state_ssm': nrm((DEPTH, DEC_BATCH, SSM_HEADS, SSM_P, SSM_N), 0.1),
        'state_ssm_conv': nrm((DEPTH, DEC_BATCH, CONV_W - 1, SSM_CONV_CH), 1.0),
        'c_prompt': nrm((BATCH, D_MODEL), 1.0),
        'c_sample': nrm((DEC_BATCH, D_MODEL), 1.0),
        'w_ada': nrm((DEPTH, D_MODEL, 6 * D_MODEL), D_MODEL ** -0.5),
        'b_ada': nrm((DEPTH, 6 * D_MODEL), 0.01),
        'norm_mix': gain((DEPTH, D_MODEL)),
        'norm_ffn': gain((DEPTH, D_MODEL)),
        'w_in': nrm((DEPTH, D_MODEL, N_IN), D_MODEL ** -0.5),
        'gdn_conv_w': nrm((DEPTH, CONV_W, GDN_CONV_CH), CONV_W ** -0.5),
        'gdn_a_log': jnp.log(jax.random.uniform(nk(), (DEPTH, GDN_HEADS), jnp.float32, minval=1.0, maxval=16.0)),
        'gdn_dt_bias': _inv_softplus_dt(nk(), (DEPTH, GDN_HEADS)),
        'gdn_norm_w': gain((DEPTH, GDN_DV)),
        'fox_f_bias': 1.0 + nrm((DEPTH, FOX_HEADS), 0.1),
        'ssm_conv_w': nrm((DEPTH, CONV_W, SSM_CONV_CH), CONV_W ** -0.5),
        'ssm_conv_b': nrm((DEPTH, SSM_CONV_CH), 0.01),
        'ssm_a_log': jnp.log(jax.random.uniform(nk(), (DEPTH, SSM_HEADS), jnp.float32, minval=1.0, maxval=16.0)),
        'ssm_dt_bias': _inv_softplus_dt(nk(), (DEPTH, SSM_HEADS)),
        'ssm_d': 1.0 + nrm((DEPTH, SSM_HEADS), 0.1),
        'ssm_norm_w': gain((DEPTH, MIX_W)),
        'w_branch': nrm((DEPTH, N_BRANCH, MIX_W, D_MODEL), MIX_W ** -0.5),
        'w_out': nrm((DEPTH, D_MODEL, D_MODEL), D_MODEL ** -0.5),
        'peer_wq': nrm((DEPTH, D_MODEL, PEER_HEADS * PEER_DQ), D_MODEL ** -0.5),
        'peer_keys': nrm((DEPTH, 2, PEER_NKEYS, PEER_DQ // 2), (PEER_DQ // 2) ** -0.5),
        'peer_u': nrm((DEPTH, PEER_NEXP, D_MODEL), D_MODEL ** -0.5),
        'peer_v': nrm((DEPTH, PEER_NEXP, D_MODEL), (PEER_HEADS * PEER_TOPK) ** -0.5),
        'final_norm': gain((D_MODEL,)),
    }


def reference(x_prompt, x_sample, cache_k, cache_v, cache_logf, page_table,
              state_gdn, state_gdn_conv, state_ssm, state_ssm_conv, c_prompt, c_sample,
              w_ada, b_ada, norm_mix, norm_ffn, w_in, gdn_conv_w, gdn_a_log, gdn_dt_bias,
              gdn_norm_w, fox_f_bias, ssm_conv_w, ssm_conv_b, ssm_a_log, ssm_dt_bias, ssm_d,
              ssm_norm_w, w_branch, w_out, peer_wq, peer_keys, peer_u, peer_v, final_norm):
    n_dec, n_pages = page_table.shape
    past = n_pages * PAGE_SIZE
    n_pr = x_prompt.shape[0]
    f32 = jnp.float32
    xp, xs = x_prompt, x_sample
    new_p, new_s = [], []
    for l in range(DEPTH):
        lw = {'w_ada': w_ada[l], 'b_ada': b_ada[l], 'norm_mix': norm_mix[l], 'norm_ffn': norm_ffn[l],
              'w_in': w_in[l], 'gdn_conv_w': gdn_conv_w[l], 'gdn_a_log': gdn_a_log[l],
              'gdn_dt_bias': gdn_dt_bias[l], 'gdn_norm_w': gdn_norm_w[l], 'fox_f_bias': fox_f_bias[l],
              'ssm_conv_w': ssm_conv_w[l], 'ssm_conv_b': ssm_conv_b[l], 'ssm_a_log': ssm_a_log[l],
              'ssm_dt_bias': ssm_dt_bias[l], 'ssm_d': ssm_d[l], 'ssm_norm_w': ssm_norm_w[l],
              'w_branch': w_branch[l], 'w_out': w_out[l], 'peer_wq': peer_wq[l],
              'peer_keys': peer_keys[l], 'peer_u': peer_u[l], 'peer_v': peer_v[l]}
        zero_states = (jnp.zeros((n_pr, GDN_HEADS, GDN_DK, GDN_DV), f32),
                       jnp.zeros((n_pr, CONV_W - 1, GDN_CONV_CH), xp.dtype),
                       jnp.zeros((n_pr, SSM_HEADS, SSM_P, SSM_N), f32),
                       jnp.zeros((n_pr, CONV_W - 1, SSM_CONV_CH), xp.dtype))
        xp, st_p = _layer(xp, c_prompt, lw, zero_states, _fox_prompt)
        new_p.append(st_p)
        attend = functools.partial(
            _fox_sample,
            k_past=cache_k[l][page_table].reshape(n_dec, past, FOX_HEADS, FOX_DH),
            v_past=cache_v[l][page_table].reshape(n_dec, past, FOX_HEADS, FOX_DH),
            logf_past=cache_logf[l][page_table].reshape(n_dec, past, FOX_HEADS))
        xs, st_s = _layer(xs, c_sample, lw,
                          (state_gdn[l], state_gdn_conv[l], state_ssm[l], state_ssm_conv[l]), attend)
        new_s.append(st_s)
    y_prompt = _rmsnorm(xp, final_norm)
    y_sample = _rmsnorm(xs, final_norm)
    k_p, v_p, f_p, g_p, gc_p, s_p, sc_p = [jnp.stack(t) for t in zip(*new_p)]
    k_s, v_s, f_s, g_s, gc_s, s_s, sc_s = [jnp.stack(t) for t in zip(*new_s)]
    return (y_prompt, y_sample, k_p, v_p, f_p, g_p, gc_p, s_p, sc_p,
            k_s, v_s, f_s, g_s, gc_s, s_s, sc_s)
```

```python
import functools
import math

import jax
import jax.numpy as jnp
from jax import lax
from jax.experimental import pallas as pl
from jax.experimental.pallas import tpu as pltpu
from jax.experimental.pallas import tpu_sc as plsc

F32 = jnp.float32
BF16 = jnp.bfloat16
HI = lax.Precision.HIGHEST

D_MODEL = 1024
DEPTH = 4
PAGE_SIZE = 128
MIX_W = D_MODEL // 2
CONV_W = 4
GDN_DK = 128
GDN_DV = 128
GDN_HEADS = MIX_W // GDN_DV
FOX_DH = 64
FOX_HEADS = MIX_W // FOX_DH
FOX_SCALE = FOX_DH ** -0.5
SSM_P = 64
SSM_HEADS = MIX_W // SSM_P
SSM_GROUPS = 2
SSM_N = 64
PEER_HEADS = 8
PEER_NKEYS = 128
PEER_DQ = 256
PEER_TOPK = 16
EPS = 1e-6

GDN_CONV_CH = 2 * GDN_HEADS * GDN_DK + GDN_HEADS * GDN_DV
SSM_CONV_CH = MIX_W + 2 * SSM_GROUPS * SSM_N
IN_SPLITS = (GDN_CONV_CH, MIX_W, GDN_HEADS, GDN_HEADS, 3 * MIX_W, FOX_HEADS,
             MIX_W, SSM_CONV_CH, SSM_HEADS, 3 * D_MODEL)

C_GATES = 0
C_AQKV = 3072
C_AGATE = 4608
C_BQKV = 5120
C_CZ = 6656
C_CXBC = 7168
C_SMALL = 7936
N_PROJ = 8192

CHUNK = 128
VMEM_LIMIT = 48 * 1024 * 1024


def _cparams(*sem):
    return pltpu.CompilerParams(dimension_semantics=sem, vmem_limit_bytes=VMEM_LIMIT)


def _pack_w_in(w_in):
    offs = [0]
    for s in IN_SPLITS:
        offs.append(offs[-1] + s)
    a_qkv, a_gate, a_beta, a_alpha, b_qkv, b_f, c_z, c_xbc, c_dt, gates = [
        w_in[:, offs[i]:offs[i + 1]] for i in range(len(IN_SPLITS))]
    small = jnp.concatenate([a_beta, a_alpha, b_f, c_dt], axis=1)
    small = jnp.pad(small, ((0, 0), (0, 128 - small.shape[1])))
    pad = jnp.zeros((w_in.shape[0], N_PROJ - C_SMALL - 128), w_in.dtype)
    out = jnp.concatenate([gates, a_qkv, a_gate, b_qkv, c_z, c_xbc, small, pad], axis=1)
    return out.astype(BF16)


def _ada_kernel(c_ref, w_ref, b_ref, o_ref):
    c = c_ref[...]
    a = (c * jax.nn.sigmoid(c)).astype(BF16)
    o_ref[...] = jnp.dot(a, w_ref[...], preferred_element_type=F32) + b_ref[...]


def _adaln(c, w, b):
    m, k = c.shape
    n = w.shape[1]
    tn = 512
    return pl.pallas_call(
        _ada_kernel,
        grid=(n // tn,),
        in_specs=[pl.BlockSpec((m, k), lambda j: (0, 0)),
                  pl.BlockSpec((k, tn), lambda j: (0, j)),
                  pl.BlockSpec((1, tn), lambda j: (0, j))],
        out_specs=pl.BlockSpec((m, tn), lambda j: (0, j)),
        out_shape=jax.ShapeDtypeStruct((m, n), F32),
        compiler_params=_cparams("parallel"),
        name="adaln",
    )(c, w, b.reshape(1, n))


def _mod_operand(m, seq_len, tm):
    bsz, d = m.shape
    if seq_len % tm == 0:
        per = seq_len // tm
        return m.reshape(bsz, 1, d), pl.BlockSpec((None, 1, d), lambda i, *_: (i // per, 0, 0))
    arr = jnp.repeat(m, seq_len, axis=0)
    return arr, pl.BlockSpec((tm, d), lambda i, *_: (i, 0))


def _rms_mod(x, nw, sc, sh):
    y = x * lax.rsqrt(jnp.mean(x * x, axis=-1, keepdims=True) + EPS)
    return (y * nw) * (1.0 + sc) + sh


def _proj_kernel(x_ref, nw_ref, sc_ref, sh_ref, w_ref, o_ref, h_scr):
    @pl.when(pl.program_id(1) == 0)
    def _():
        h_scr[...] = _rms_mod(x_ref[...], nw_ref[...], sc_ref[...], sh_ref[...]).astype(BF16)

    o_ref[...] = jnp.dot(h_scr[...], w_ref[...], preferred_element_type=F32)


def _proj_in(x, nw, sc, sh, w, seq_len):
    t, d = x.shape
    n = w.shape[1]
    tm = min(512, t)
    tn = 1024
    sc_a, sc_s = _mod_operand(sc, seq_len, tm)
    sh_a, sh_s = _mod_operand(sh, seq_len, tm)
    return pl.pallas_call(
        _proj_kernel,
        grid=(t // tm, n // tn),
        in_specs=[pl.BlockSpec((tm, d), lambda i, j: (i, 0)),
                  pl.BlockSpec((1, d), lambda i, j: (0, 0)),
                  sc_s, sh_s,
                  pl.BlockSpec((d, tn), lambda i, j: (0, j))],
        out_specs=pl.BlockSpec((tm, tn), lambda i, j: (i, j)),
        out_shape=jax.ShapeDtypeStruct((t, n), F32),
        scratch_shapes=[pltpu.VMEM((tm, d), BF16)],
        compiler_params=_cparams("parallel", "arbitrary"),
        name="proj_in",
    )(x, nw.reshape(1, d), sc_a, sh_a, w)


def _cumsum_maps(g, c):
    row = lax.broadcasted_iota(jnp.int32, (c, c), 0)
    col = lax.broadcasted_iota(jnp.int32, (c, c), 1)
    tril = row >= col
    gb = jnp.broadcast_to(g, (c, c))
    ri = jnp.dot(tril.astype(F32), gb, precision=HI, preferred_element_type=F32)
    rj = jnp.dot(jnp.ones((c, c), F32), jnp.where(row <= col, gb, 0.0), precision=HI,
                 preferred_element_type=F32)
    return ri, rj, tril, row, col


def _dot(a, b):
    return jnp.dot(a, b, precision=HI, preferred_element_type=F32)


def _dot_nt(a, b):
    return lax.dot_general(a, b, (((1,), (1,)), ((), ())), precision=HI,
                           preferred_element_type=F32)


def _gdn_kernel(q_ref, k_ref, v_ref, gb_ref, s0_ref, o_ref, sout_ref, s_scr, *, c, levels):
    ci = pl.program_id(2)

    @pl.when(ci == 0)
    def _():
        s_scr[...] = s0_ref[...]

    q = q_ref[...]
    k = k_ref[...]
    v = v_ref[...]
    g = gb_ref[:, 0:1]
    beta = gb_ref[:, 1:2]
    ri, rj, tril, row, col = _cumsum_maps(g, c)
    decay = jnp.where(tril, jnp.exp(jnp.where(tril, ri - rj, 0.0)), 0.0)
    gc = ri[:, 0:1]
    gc_last = ri[c - 1:c, 0:1]
    kb = k * beta
    m = jnp.where(row > col, _dot_nt(kb, k) * decay, 0.0)
    eye = (row == col).astype(F32)
    blk = (row // 8) == (col // 8)
    p = jnp.where(blk, m, 0.0)
    x = eye - p
    for _ in range(levels[0]):
        p = _dot(p, p)
        x = x + _dot(x, p)
    b = 8
    for _ in range(levels[1]):
        low = jnp.where(((row // (2 * b)) == (col // (2 * b))) & ((row // b) != (col // b)), m, 0.0)
        x = x - _dot(_dot(x, low), x)
        b *= 2
    u_in = _dot(x, v * beta)
    wk_in = _dot(x, kb * jnp.exp(gc))
    qk = _dot_nt(q, k) * decay
    q_dec = q * jnp.exp(gc)
    k_dec = k * jnp.exp(gc_last - gc)
    s = s_scr[...]
    u = u_in - _dot(wk_in, s)
    o_ref[...] = _dot(q_dec, s) + _dot(qk, u)
    s_new = s * jnp.exp(gc_last) + _dot(k_dec.T, u)
    s_scr[...] = s_new
    sout_ref[...] = s_new


def _gdn_scan(q, k, v, gb, s0, n_valid):
    b, h, l, dk = q.shape
    c = CHUNK
    n_eff = min(n_valid, c)
    base = min(n_eff, 8)
    levels = (max(0, math.ceil(math.log2(base)) - 1) if base > 1 else 0,
              max(0, math.ceil(math.log2(n_eff / 8))) if n_eff > 8 else 0)
    blk = pl.BlockSpec((None, None, c, dk), lambda bi, hi, ci: (bi, hi, ci, 0))
    st = pl.BlockSpec((None, None, dk, dk), lambda bi, hi, ci: (bi, hi, 0, 0))
    return pl.pallas_call(
        functools.partial(_gdn_kernel, c=c, levels=levels),
        grid=(b, h, l // c),
        in_specs=[blk, blk, blk,
                  pl.BlockSpec((None, None, c, 8), lambda bi, hi, ci: (bi, hi, ci, 0)),
                  st],
        out_specs=[blk, st],
        out_shape=[jax.ShapeDtypeStruct((b, h, l, dk), F32),
                   jax.ShapeDtypeStruct((b, h, dk, dk), F32)],
        scratch_shapes=[pltpu.VMEM((dk, dk), F32)],
        compiler_params=_cparams("parallel", "parallel", "arbitrary"),
        name="gdn_scan",
    )(q, k, v, gb, s0)


def _ssd_kernel(x_ref, bc_ref, da_ref, h0_ref, y_ref, hout_ref, h_scr, *, c):
    pi = pl.program_id(1)
    ci = pl.program_id(2)

    @pl.when(ci == 0)
    def _():
        h_scr[...] = h0_ref[...]

    grp = pi // 2
    lane = lax.broadcasted_iota(jnp.int32, (c, 128), 1)
    lane_lo = lane < 64
    gmask = ((lane >= 64) == (grp == 1)).astype(F32)
    x = x_ref[...]
    bm = bc_ref[:, 0:128] * gmask
    cm = bc_ref[:, 128:256] * gmask
    hst = h_scr[...]
    cb = _dot_nt(cm, bm)
    y = jnp.zeros((c, 128), F32)
    hrow = lax.broadcasted_iota(jnp.int32, (128, 128), 0) < 64
    h_new = jnp.zeros((128, 128), F32)
    for hh in range(2):
        dt = da_ref[:, 2 * hh:2 * hh + 1]
        da = da_ref[:, 2 * hh + 1:2 * hh + 2]
        ri, rj, tril, _, _ = _cumsum_maps(da, c)
        seg = jnp.where(tril, jnp.exp(jnp.where(tril, ri - rj, 0.0)), 0.0)
        a = ri[:, 0:1]
        a_last = ri[c - 1:c, 0:1]
        hm = lane_lo if hh == 0 else jnp.logical_not(lane_lo)
        xdt = jnp.where(hm, x, 0.0) * dt
        y_h = _dot(cb * seg, xdt) + _dot_nt(cm * jnp.exp(a), hst)
        y = y + jnp.where(hm, y_h, 0.0)
        upd = _dot(xdt.T, bm * jnp.exp(a_last - a))
        hsel = hrow if hh == 0 else jnp.logical_not(hrow)
        h_new = h_new + jnp.where(hsel, hst * jnp.exp(a_last) + upd, 0.0)
    y_ref[...] = y
    h_scr[...] = h_new
    hout_ref[...] = h_new


def _ssd_scan(x, bc, da, h0):
    b, l, _ = x.shape
    c = CHUNK
    st = pl.BlockSpec((None, None, 128, 128), lambda bi, pi, ci: (bi, pi, 0, 0))
    return pl.pallas_call(
        functools.partial(_ssd_kernel, c=c),
        grid=(b, 4, l // c),
        in_specs=[pl.BlockSpec((None, c, 128), lambda bi, pi, ci: (bi, ci, pi)),
                  pl.BlockSpec((None, c, 256), lambda bi, pi, ci: (bi, ci, 0)),
                  pl.BlockSpec((None, None, c, 8), lambda bi, pi, ci: (bi, pi, ci, 0)),
                  st],
        out_specs=[pl.BlockSpec((None, c, 128), lambda bi, pi, ci: (bi, ci, pi)), st],
        out_shape=[jax.ShapeDtypeStruct((b, l, 512), F32),
                   jax.ShapeDtypeStruct((b, 4, 128, 128), F32)],
        scratch_shapes=[pltpu.VMEM((128, 128), F32)],
        compiler_params=_cparams("parallel", "parallel", "arbitrary"),
        name="ssd_scan",
    )(x, bc, da, h0)


def _fox_prompt_kernel(q_ref, k_ref, v_ref, cq_ref, ck_ref, o_ref, kb_scr, vb_scr,
                       m_scr, l_scr, acc_scr, *, tq, tk):
    i = pl.program_id(2)

    @pl.when(i == 0)
    def _():
        kb_scr[...] = k_ref[...].astype(BF16)
        vb_scr[...] = v_ref[...].astype(BF16)

    lane = lax.broadcasted_iota(jnp.int32, (tq, 128), 1)
    q = q_ref[...] * FOX_SCALE
    rowp = lax.broadcasted_iota(jnp.int32, (tq, tk), 0)
    colp = lax.broadcasted_iota(jnp.int32, (tq, tk), 1)
    outs = []
    for hh in range(2):
        hm = (lane < 64) if hh == 0 else (lane >= 64)
        qh = jnp.where(hm, q, 0.0).astype(BF16)
        cq = cq_ref[:, hh:hh + 1]
        m_scr[...] = jnp.full((tq, 128), -jnp.inf, F32)
        l_scr[...] = jnp.zeros((tq, 128), F32)
        acc_scr[...] = jnp.zeros((tq, 128), F32)

        def step(j, masked, qh=qh, cq=cq, hh=hh):
            off = pl.multiple_of(j * tk, tk)
            kk = kb_scr[pl.ds(off, tk), :]
            vv = vb_scr[pl.ds(off, tk), :]
            ck = ck_ref[j, hh:hh + 1, :]
            s = lax.dot_general(qh, kk, (((1,), (1,)), ((), ())), preferred_element_type=F32)
            s = s + cq - ck
            if masked:
                s = jnp.where(colp <= rowp, s, -jnp.inf)
            m_prev = m_scr[:, 0:1]
            m_new = jnp.maximum(m_prev, jnp.max(s, axis=1, keepdims=True))
            alpha = jnp.exp(m_prev - m_new)
            p = jnp.exp(s - m_new)
            l_new = alpha * l_scr[:, 0:1] + jnp.sum(p, axis=1, keepdims=True)
            acc_scr[...] = alpha * acc_scr[...] + jnp.dot(p.astype(BF16), vv,
                                                          preferred_element_type=F32)
            m_scr[...] = jnp.broadcast_to(m_new, (tq, 128))
            l_scr[...] = jnp.broadcast_to(l_new, (tq, 128))

        def body(j, carry):
            step(j, False)
            return carry

        lax.fori_loop(0, i, body, 0)
        step(i, True)
        outs.append((acc_scr[...] / l_scr[...], hm))
    o_ref[...] = jnp.where(outs[0][1], outs[0][0], outs[1][0]).astype(o_ref.dtype)


def _fox_prompt(proj, cum, bsz, seq):
    tq = tk = 512 if seq % 512 == 0 else seq
    nq = seq // tq
    cq = cum.reshape(bsz, seq, 4, 2).transpose(0, 2, 1, 3)
    ck = cum.reshape(bsz, nq, tk, 4, 2).transpose(0, 3, 1, 4, 2)
    qb = C_BQKV // 128
    kb = (C_BQKV + 512) // 128
    vb = (C_BQKV + 1024) // 128
    return pl.pallas_call(
        functools.partial(_fox_prompt_kernel, tq=tq, tk=tk),
        grid=(bsz, 4, nq),
        in_specs=[pl.BlockSpec((tq, 128), lambda b, p, i: (b * nq + i, qb + p)),
                  pl.BlockSpec((seq, 128), lambda b, p, i: (b, kb + p)),
                  pl.BlockSpec((seq, 128), lambda b, p, i: (b, vb + p)),
                  pl.BlockSpec((None, None, tq, 2), lambda b, p, i: (b, p, i, 0)),
                  pl.BlockSpec((None, None, nq, 2, tk), lambda b, p, i: (b, p, 0, 0, 0))],
        out_specs=pl.BlockSpec((tq, 128), lambda b, p, i: (b * nq + i, p)),
        out_shape=jax.ShapeDtypeStruct((bsz * seq, 512), BF16),
        scratch_shapes=[pltpu.VMEM((seq, 128), BF16), pltpu.VMEM((seq, 128), BF16),
                        pltpu.VMEM((tq, 128), F32), pltpu.VMEM((tq, 128), F32),
                        pltpu.VMEM((tq, 128), F32)],
        compiler_params=_cparams("parallel", "parallel", "arbitrary"),
        name="fox_prompt",
    )(proj, proj, proj, cq, ck)


def _fox_sample_kernel(pt_ref, q_ref, kn_ref, vn_ref, ln_ref, kp_ref, vp_ref, lp_ref, o_ref,
                       m_scr, l_scr, acc_scr, cs_scr, suf_scr, *, n_new):
    s_id = pl.program_id(1)
    n_steps = pl.num_programs(1)
    r = 8 * n_new
    rowi = lax.broadcasted_iota(jnp.int32, (r, 512), 0)
    lanei = lax.broadcasted_iota(jnp.int32, (r, 512), 1)
    hmask = (lanei // FOX_DH) == (rowi % 8)
    q = q_ref[...] * FOX_SCALE
    qrep = jnp.concatenate([jnp.broadcast_to(q[t:t + 1, :], (8, 512)) for t in range(n_new)], axis=0)
    qbd = jnp.where(hmask, qrep, 0.0).astype(BF16)
    krow = lax.broadcasted_iota(jnp.int32, (128, 128), 0)
    kcol = lax.broadcasted_iota(jnp.int32, (128, 128), 1)
    rr = lax.broadcasted_iota(jnp.int32, (r, 128), 0)
    kk = lax.broadcasted_iota(jnp.int32, (r, 128), 1)

    def attend(kx, vx, bias, valid):
        s = lax.dot_general(qbd, kx.astype(BF16), (((1,), (1,)), ((), ())),
                            preferred_element_type=F32)
        s = s + bias
        if valid is not None:
            s = jnp.where(valid, s, -jnp.inf)
        m_prev = m_scr[:, 0:1]
        m_new = jnp.maximum(m_prev, jnp.max(s, axis=1, keepdims=True))
        alpha = jnp.exp(m_prev - m_new)
        p = jnp.exp(s - m_new)
        l_new = alpha * l_scr[:, 0:1] + jnp.sum(p, axis=1, keepdims=True)
        acc_scr[...] = alpha * acc_scr[...] + jnp.dot(p.astype(BF16), vx.astype(BF16),
                                                      preferred_element_type=F32)
        m_scr[...] = jnp.broadcast_to(m_new, (r, 128))
        l_scr[...] = jnp.broadcast_to(l_new, (r, 128))

    @pl.when(s_id == 0)
    def _():
        m_scr[...] = jnp.full((r, 128), -jnp.inf, F32)
        l_scr[...] = jnp.zeros((r, 128), F32)
        acc_scr[...] = jnp.zeros((r, 512), F32)
        suf_scr[...] = jnp.zeros((r, 128), F32)
        lf = jnp.concatenate([ln_ref[...]] * n_new, axis=0)
        cum = _dot(lf, (krow <= kcol).astype(F32))
        tq = rr // 8
        cs = jnp.sum(jnp.where(kk == tq, cum, 0.0), axis=1, keepdims=True)
        cs_scr[...] = jnp.broadcast_to(cs, (r, 128))
        attend(kn_ref[...], vn_ref[...], cs - cum, kk <= tq)

    @pl.when(s_id > 0)
    def _():
        lf = jnp.concatenate([lp_ref[...]] * n_new, axis=0)
        after = _dot(lf, (krow > kcol).astype(F32))
        suf = suf_scr[:, 0:1]
        attend(kp_ref[...], vp_ref[...], cs_scr[:, 0:1] + suf + after, None)
        suf_scr[...] = jnp.broadcast_to(suf + jnp.sum(lf, axis=1, keepdims=True), (r, 128))

    @pl.when(s_id == n_steps - 1)
    def _():
        o = jnp.where(hmask, acc_scr[...] / l_scr[:, 0:1], 0.0)
        o_ref[...] = jnp.sum(o.reshape(n_new, 8, 512), axis=1).astype(o_ref.dtype)


def _fox_sample(q, k_new, v_new, lf_new, cache_k, cache_v, cache_lf_t, page_table):
    bd, n_new, _ = q.shape
    n_pages = page_table.shape[1]

    def page(b, s, pt):
        return pt[b, n_pages - jnp.maximum(s, 1)]

    grid_spec = pltpu.PrefetchScalarGridSpec(
        num_scalar_prefetch=1,
        grid=(bd, n_pages + 1),
        in_specs=[pl.BlockSpec((None, n_new, 512), lambda b, s, pt: (b, 0, 0)),
                  pl.BlockSpec((None, 128, 512), lambda b, s, pt: (b, 0, 0)),
                  pl.BlockSpec((None, 128, 512), lambda b, s, pt: (b, 0, 0)),
                  pl.BlockSpec((None, 8, 128), lambda b, s, pt: (b, 0, 0)),
                  pl.BlockSpec((None, 128, 512), lambda b, s, pt: (page(b, s, pt), 0, 0)),
                  pl.BlockSpec((None, 128, 512), lambda b, s, pt: (page(b, s, pt), 0, 0)),
                  pl.BlockSpec((None, 8, 128), lambda b, s, pt: (page(b, s, pt), 0, 0))],
        out_specs=pl.BlockSpec((None, n_new, 512), lambda b, s, pt: (b, 0, 0)),
        scratch_shapes=[pltpu.VMEM((8 * n_new, 128), F32), pltpu.VMEM((8 * n_new, 128), F32),
                        pltpu.VMEM((8 * n_new, 512), F32), pltpu.VMEM((8 * n_new, 128), F32),
                        pltpu.VMEM((8 * n_new, 128), F32)])
    return pl.pallas_call(
        functools.partial(_fox_sample_kernel, n_new=n_new),
        grid_spec=grid_spec,
        out_shape=jax.ShapeDtypeStruct((bd, n_new, 512), F32),
        compiler_params=_cparams("parallel", "arbitrary"),
        name="fox_sample",
    )(page_table, q, k_new, v_new, lf_new, cache_k, cache_v, cache_lf_t)


def _merge_kernel(oa_ref, ob_ref, oc_ref, g0_ref, g1_ref, g2_ref, wb_ref, wo_ref, x_ref,
                  gate_ref, nw_ref, sc_ref, sh_ref, xo_ref, h_ref):
    acc = None
    for br, (o_r, g_r) in enumerate(((oa_ref, g0_ref), (ob_ref, g1_ref), (oc_ref, g2_ref))):
        y = jnp.dot(o_r[...].astype(BF16), wb_ref[br], preferred_element_type=F32)
        y = jax.nn.sigmoid(g_r[...]) * y
        acc = y if acc is None else acc + y
    out = jnp.dot(acc.astype(BF16), wo_ref[...], preferred_element_type=F32)
    xn = x_ref[...] + gate_ref[...] * out
    xo_ref[...] = xn
    h_ref[...] = _rms_mod(xn, nw_ref[...], sc_ref[...], sh_ref[...])


def _merge(o_a, o_b, o_c, proj, wb, wo, x, g1, nw, sc2, sh2, seq_len):
    t, d = x.shape
    tm = min(256, t)
    g_a, g_s = _mod_operand(g1, seq_len, tm)
    sc_a, sc_s = _mod_operand(sc2, seq_len, tm)
    sh_a, sh_s = _mod_operand(sh2, seq_len, tm)
    br = pl.BlockSpec((tm, MIX_W), lambda i: (i, 0))
    row = pl.BlockSpec((tm, d), lambda i: (i, 0))
    return pl.pallas_call(
        _merge_kernel,
        grid=(t // tm,),
        in_specs=[br, br, br,
                  pl.BlockSpec((tm, d), lambda i: (i, 0)),
                  pl.BlockSpec((tm, d), lambda i: (i, 1)),
                  pl.BlockSpec((tm, d), lambda i: (i, 2)),
                  pl.BlockSpec((3, MIX_W, d), lambda i: (0, 0, 0)),
                  pl.BlockSpec((d, d), lambda i: (0, 0)),
                  row, g_s,
                  pl.BlockSpec((1, d), lambda i: (0, 0)),
                  sc_s, sh_s],
        out_specs=[row, row],
        out_shape=[jax.ShapeDtypeStruct((t, d), F32), jax.ShapeDtypeStruct((t, d), F32)],
        compiler_params=_cparams("parallel"),
        name="merge",
    )(o_a, o_b, o_c, proj, proj, proj, wb, wo, x, g_a, nw.reshape(1, d), sc_a, sh_a)


def _top16(s, width):
    rows = s.shape[0]
    pos = lax.broadcasted_iota(jnp.int32, (rows, width), 1).astype(F32)
    out_lane = lax.broadcasted_iota(jnp.int32, (rows, 128), 1)
    vals = jnp.zeros((rows, 128), F32)
    idxs = jnp.zeros((rows, 128), F32)
    for it in range(PEER_TOPK):
        mx = jnp.max(s, axis=1, keepdims=True)
        ix = jnp.min(jnp.where(s == mx, pos, float(width)), axis=1, keepdims=True)
        vals = jnp.where(out_lane == it, mx, vals)
        idxs = jnp.where(out_lane == it, ix, idxs)
        s = jnp.where(pos == ix, -jnp.inf, s)
    return vals, idxs


def _route_kernel(h_ref, wq_ref, keys_ref, eid_ref, gate_ref, *, tb):
    q = jnp.dot(h_ref[...].astype(BF16), wq_ref[...], preferred_element_type=F32)
    lane128 = lax.broadcasted_iota(jnp.int32, (128, 256), 0)
    cand_i = lax.broadcasted_iota(jnp.int32, (128, 256), 1)
    e_hi = (cand_i // PEER_TOPK == lane128).astype(F32)
    e_lo = (cand_i % PEER_TOPK == lane128).astype(F32)
    out_lane = lax.broadcasted_iota(jnp.int32, (tb, 128), 1)
    cpos = lax.broadcasted_iota(jnp.int32, (tb, 256), 1).astype(F32)
    eid_all = jnp.zeros((tb, 128), F32)
    top_all = jnp.zeros((tb, 128), F32)
    for hd in range(PEER_HEADS):
        sv, si = [], []
        for half in range(2):
            c0 = (hd * 2 + half) * 128
            sc = jnp.dot(q[:, c0:c0 + 128].astype(BF16), keys_ref[half],
                         preferred_element_type=F32)
            v_, i_ = _top16(sc, 128)
            sv.append(v_)
            si.append(i_)
        cand = _dot(sv[0], e_hi) + _dot(sv[1], e_lo)
        cid = _dot(si[0], e_hi) * float(PEER_NKEYS) + _dot(si[1], e_lo)
        for it in range(PEER_TOPK):
            mx = jnp.max(cand, axis=1, keepdims=True)
            ix = jnp.min(jnp.where(cand == mx, cpos, 256.0), axis=1, keepdims=True)
            hit = cpos == ix
            ev = jnp.sum(jnp.where(hit, cid, 0.0), axis=1, keepdims=True)
            slot = out_lane == hd * PEER_TOPK + it
            top_all = jnp.where(slot, mx, top_all)
            eid_all = jnp.where(slot, ev, eid_all)
            cand = jnp.where(hit, -jnp.inf, cand)
    gi = lax.broadcasted_iota(jnp.int32, (128, 128), 0)
    gj = lax.broadcasted_iota(jnp.int32, (128, 128), 1)
    first = (gi == (gj // PEER_TOPK) * PEER_TOPK).astype(F32)
    same = (gi // PEER_TOPK == gj // PEER_TOPK).astype(F32)
    e = jnp.exp(top_all - _dot(top_all, first))
    gate_all = e / _dot(e, same)
    eid_ref[...] = eid_all.astype(jnp.int32)
    for sblk in range(tb // 128):
        gate_ref[sblk] = gate_all[sblk * 128:(sblk + 1) * 128, :].T


def _route(h, wq, keys_t):
    t, d = h.shape
    tb = 128
    return pl.pallas_call(
        functools.partial(_route_kernel, tb=tb),
        grid=(t // tb,),
        in_specs=[pl.BlockSpec((tb, d), lambda i: (i, 0)),
                  pl.BlockSpec((d, PEER_HEADS * PEER_DQ), lambda i: (0, 0)),
                  pl.BlockSpec((2, 128, 128), lambda i: (0, 0, 0))],
        out_specs=[pl.BlockSpec((tb, 128), lambda i: (i, 0)),
                   pl.BlockSpec((tb // 128, 128, 128), lambda i: (i, 0, 0))],
        out_shape=[jax.ShapeDtypeStruct((t, 128), jnp.int32),
                   jax.ShapeDtypeStruct((t // 128, 128, 128), F32)],
        compiler_params=_cparams("parallel"),
        name="peer_route",
    )(h, wq, keys_t)


GATHER_WINDOW = 64


def _sc_gather(table, idx):
    n = idx.shape[0]
    width = table.shape[1]
    mesh = plsc.VectorSubcoreMesh(core_axis_name="core", subcore_axis_name="subcore")
    n_workers = mesh.num_cores * mesh.num_subcores
    win = GATHER_WINDOW
    per_w = n // n_workers
    n_chunks = per_w // win
    assert per_w * n_workers == n and n_chunks * win == per_w and n_chunks % 2 == 0

    @pl.kernel(out_type=jax.ShapeDtypeStruct((n, width), table.dtype), mesh=mesh,
               scratch_types=[pltpu.VMEM((win,), jnp.int32), pltpu.VMEM((win,), jnp.int32),
                              pltpu.VMEM((win, width), table.dtype),
                              pltpu.VMEM((win, width), table.dtype),
                              pltpu.SemaphoreType.DMA, pltpu.SemaphoreType.DMA,
                              pltpu.SemaphoreType.DMA, pltpu.SemaphoreType.DMA],
               name="peer_gather")
    def gather_kernel(tab_hbm, idx_hbm, out_hbm, i0, i1, r0, r1, g0, g1, w0, w1):
        wid = lax.axis_index("subcore") * mesh.num_cores + lax.axis_index("core")
        base = wid * per_w
        bufs = ((i0, r0, g0, w0), (i1, r1, g1, w1))

        def write_copy(rv, ws, off):
            return pltpu.make_async_copy(rv, out_hbm.at[pl.ds(off, win)], ws)

        @pl.loop(0, n_chunks, step=2)
        def _(c):
            for b, (iv, rv, gs, ws) in enumerate(bufs):
                off = base + (c + b) * win

                @pl.when(c > 0)
                def _():
                    write_copy(rv, ws, off).wait()

                pltpu.sync_copy(idx_hbm.at[pl.ds(off, win)], iv)
                pltpu.make_async_copy(tab_hbm.at[iv], rv, gs).start()
            for b, (iv, rv, gs, ws) in enumerate(bufs):
                off = base + (c + b) * win
                pltpu.make_async_copy(tab_hbm.at[iv], rv, gs).wait()
                write_copy(rv, ws, off).start()

        for iv, rv, gs, ws in bufs:
            write_copy(rv, ws, base).wait()

    return gather_kernel(table, idx)


def _pack_table(tab):
    b = lax.bitcast_convert_type(tab.astype(BF16), jnp.uint16).astype(jnp.uint32)
    return lax.bitcast_convert_type(b[:, :512] | (b[:, 512:] << 16), jnp.int32)


def _unpack(w):
    lo = pltpu.bitcast(w << 16, F32)
    hi = pltpu.bitcast(w & jnp.int32(-65536), F32)
    return lo, hi


def _peer_kernel(gu_ref, gv_ref, h_ref, gt_ref, x_ref, g2_ref, o_ref, *, tb):
    base = (pl.program_id(0) * tb) % 128
    gt = gt_ref[...]
    tok = lax.broadcasted_iota(jnp.int32, (128, 128), 1)
    for t in range(tb):
        h_lo = h_ref[t:t + 1, 0:512]
        h_hi = h_ref[t:t + 1, 512:1024]
        u_lo, u_hi = _unpack(gu_ref[t * 128:(t + 1) * 128, :])
        pu = u_lo * h_lo + u_hi * h_hi
        p128 = pu[:, 0:128] + pu[:, 128:256] + pu[:, 256:384] + pu[:, 384:512]
        pre = jnp.sum(p128, axis=1, keepdims=True)
        act = 0.5 * pre * (1.0 + lax.erf(pre * (2.0 ** -0.5)))
        gcol = jnp.sum(jnp.where(tok == base + t, gt, 0.0), axis=1, keepdims=True)
        w = gcol * act
        v_lo, v_hi = _unpack(gv_ref[t * 128:(t + 1) * 128, :])
        y_lo = jnp.sum(v_lo * w, axis=0, keepdims=True)
        y_hi = jnp.sum(v_hi * w, axis=0, keepdims=True)
        o_ref[t:t + 1, 0:512] = x_ref[t:t + 1, 0:512] + g2_ref[t:t + 1, 0:512] * y_lo
        o_ref[t:t + 1, 512:1024] = x_ref[t:t + 1, 512:1024] + g2_ref[t:t + 1, 512:1024] * y_hi


def _peer_mix(gu, gv, h, gate_t, x, g2, seq_len):
    t, d = x.shape
    tb = 16
    if seq_len % tb == 0:
        per = seq_len // tb
        g_a = jnp.broadcast_to(g2.reshape(g2.shape[0], 1, d), (g2.shape[0], tb, d))
        g_s = pl.BlockSpec((None, tb, d), lambda i: (i // per, 0, 0))
    else:
        g_a = jnp.repeat(g2, seq_len, axis=0)
        g_s = pl.BlockSpec((tb, d), lambda i: (i, 0))
    row = pl.BlockSpec((tb, d), lambda i: (i, 0))
    rows = pl.BlockSpec((tb * 128, 512), lambda i: (i, 0))
    per_grp = 128 // tb
    return pl.pallas_call(
        functools.partial(_peer_kernel, tb=tb),
        grid=(t // tb,),
        in_specs=[rows, rows, row,
                  pl.BlockSpec((None, 128, 128), lambda i: (i // per_grp, 0, 0)),
                  row, g_s],
        out_specs=row,
        out_shape=jax.ShapeDtypeStruct((t, d), F32),
        compiler_params=_cparams("parallel"),
        name="peer_mix",
    )(gu, gv, h, gate_t, x, g_a)


def _final_norm_kernel(x_ref, w_ref, o_ref):
    x = x_ref[...]
    o_ref[...] = x * lax.rsqrt(jnp.mean(x * x, axis=-1, keepdims=True) + EPS) * w_ref[...]


def _final_norm(x, w):
    t, d = x.shape
    tm = min(512, t)
    return pl.pallas_call(
        _final_norm_kernel,
        grid=(t // tm,),
        in_specs=[pl.BlockSpec((tm, d), lambda i: (i, 0)), pl.BlockSpec((1, d), lambda i: (0, 0))],
        out_specs=pl.BlockSpec((tm, d), lambda i: (i, 0)),
        out_shape=jax.ShapeDtypeStruct((t, d), F32),
        compiler_params=_cparams("parallel"),
        name="final_norm",
    )(x, w.reshape(1, d))


def _causal_conv(x, buf, w):
    l = x.shape[1]
    xp = jnp.concatenate([buf.astype(x.dtype), x], axis=1)
    y = sum(xp[:, i:i + l] * w[i] for i in range(CONV_W))
    return y, xp[:, l:]


def _l2norm(x):
    return x * lax.rsqrt(jnp.sum(x * x, axis=-1, keepdims=True) + EPS)


def _pad_seq(a, axis, lp):
    pad = lp - a.shape[axis]
    if pad == 0:
        return a
    cfg = [(0, 0)] * a.ndim
    cfg[axis] = (0, pad)
    return jnp.pad(a, cfg)


def _layer(x, mod, lw, states, bsz, seq, fox):
    sh1, sc1, g1, sh2, sc2, g2 = mod
    gdn_s, gdn_b, ssm_s, ssm_b = states
    t = bsz * seq
    lp = -(-seq // CHUNK) * CHUNK
    proj = _proj_in(x, lw["norm_mix"], sc1, sh1, lw["w_in"], seq)
    p3 = proj.reshape(bsz, seq, N_PROJ)
    small = p3[:, :, C_SMALL:C_SMALL + 24]
    a_beta, a_alpha, b_f, c_dt = small[..., 0:4], small[..., 4:8], small[..., 8:16], small[..., 16:24]

    a_conv, gdn_b_new = _causal_conv(p3[:, :, C_AQKV:C_AQKV + GDN_CONV_CH], gdn_b, lw["gdn_conv_w"])
    a_act = jax.nn.silu(a_conv)
    q = _l2norm(a_act[..., 0:512].reshape(bsz, seq, GDN_HEADS, GDN_DK)) * GDN_DK ** -0.5
    k = _l2norm(a_act[..., 512:1024].reshape(bsz, seq, GDN_HEADS, GDN_DK))
    v = a_act[..., 1024:1536].reshape(bsz, seq, GDN_HEADS, GDN_DV)
    beta = jax.nn.sigmoid(a_beta)
    g = -jnp.exp(lw["gdn_a_log"]) * jax.nn.softplus(a_alpha + lw["gdn_dt_bias"])
    gb = jnp.stack([g, beta], axis=-1)
    gb = jnp.pad(gb, ((0, 0), (0, 0), (0, 0), (0, 6))).transpose(0, 2, 1, 3)
    hm = lambda a: _pad_seq(a.transpose(0, 2, 1, 3), 2, lp)
    o_a, gdn_s_new = _gdn_scan(hm(q), hm(k), hm(v), _pad_seq(gb, 2, lp), gdn_s, seq)
    o_a = o_a[:, :, :seq].transpose(0, 2, 1, 3)
    o_a = o_a * lax.rsqrt(jnp.mean(o_a * o_a, axis=-1, keepdims=True) + EPS) * lw["gdn_norm_w"]
    o_a = o_a * jax.nn.silu(p3[:, :, C_AGATE:C_AGATE + MIX_W].reshape(bsz, seq, GDN_HEADS, GDN_DV))
    o_a = o_a.reshape(t, MIX_W).astype(BF16)

    logf = jax.nn.log_sigmoid(b_f + lw["fox_f_bias"])
    kf = p3[:, :, C_BQKV + 512:C_BQKV + 1024].reshape(bsz, seq, FOX_HEADS, FOX_DH)
    vf = p3[:, :, C_BQKV + 1024:C_BQKV + 1536].reshape(bsz, seq, FOX_HEADS, FOX_DH)
    o_b = fox(proj, p3, logf)

    xbc, ssm_b_new = _causal_conv(p3[:, :, C_CXBC:C_CXBC + SSM_CONV_CH], ssm_b, lw["ssm_conv_w"])
    xbc = jax.nn.silu(xbc + lw["ssm_conv_b"])
    dt = jax.nn.softplus(c_dt + lw["ssm_dt_bias"])
    da = dt * (-jnp.exp(lw["ssm_a_log"]))
    dd = jnp.stack([dt, da], axis=-1).reshape(bsz, seq, 4, 4)
    dd = jnp.pad(dd, ((0, 0), (0, 0), (0, 0), (0, 4))).transpose(0, 2, 1, 3)
    xs = xbc[..., 0:MIX_W]
    y, ssm_s_new = _ssd_scan(_pad_seq(xs, 1, lp), _pad_seq(xbc[..., MIX_W:], 1, lp),
                             _pad_seq(dd, 2, lp), ssm_s)
    y = y[:, :seq] + jnp.repeat(lw["ssm_d"], SSM_P)[None, None, :] * xs
    yz = y * jax.nn.silu(p3[:, :, C_CZ:C_CZ + MIX_W])
    o_c = yz * lax.rsqrt(jnp.mean(yz * yz, axis=-1, keepdims=True) + EPS) * lw["ssm_norm_w"]
    o_c = o_c.reshape(t, MIX_W).astype(BF16)

    x, h2 = _merge(o_a, o_b, o_c, proj, lw["w_branch"], lw["w_out"], x, g1, lw["norm_ffn"],
                   sc2, sh2, seq)
    eid, gate_t = _route(h2, lw["peer_wq"], lw["peer_keys_t"])
    flat = eid.reshape(-1)
    gu = _sc_gather(lw["peer_u"], flat)
    gv = _sc_gather(lw["peer_v"], flat)
    x = _peer_mix(gu, gv, h2, gate_t, x, g2, seq)
    return x, (kf, vf, logf, gdn_s_new, gdn_b_new, ssm_s_new, ssm_b_new)


def _ssm_state_pack(s):
    b = s.shape[0]
    sp = s.reshape(b, 4, 128, SSM_N)
    z = jnp.zeros_like(sp)
    lo = jnp.concatenate([sp, z], axis=-1)
    hi = jnp.concatenate([z, sp], axis=-1)
    grp = (jnp.arange(4) // 2).reshape(1, 4, 1, 1)
    return jnp.where(grp == 0, lo, hi)


def _ssm_state_unpack(sp):
    b = sp.shape[0]
    out = jnp.where((jnp.arange(4) // 2).reshape(1, 4, 1, 1) == 0, sp[..., :SSM_N], sp[..., SSM_N:])
    return out.reshape(b, SSM_HEADS, SSM_P, SSM_N)


def kernel(x_prompt, x_sample, cache_k, cache_v, cache_logf, page_table, state_gdn, state_gdn_conv,
           state_ssm, state_ssm_conv, c_prompt, c_sample, w_ada, b_ada, norm_mix, norm_ffn, w_in,
           gdn_conv_w, gdn_a_log, gdn_dt_bias, gdn_norm_w, fox_f_bias, ssm_conv_w, ssm_conv_b,
           ssm_a_log, ssm_dt_bias, ssm_d, ssm_norm_w, w_branch, w_out, peer_wq, peer_keys, peer_u,
           peer_v, final_norm):
    n_pr, seq, d = x_prompt.shape
    n_dec, dec_seq, _ = x_sample.shape
    n_pool = cache_k.shape[1]
    xp = x_prompt.reshape(n_pr * seq, d)
    xs = x_sample.reshape(n_dec * dec_seq, d)
    c_all = jnp.concatenate([c_prompt, c_sample], axis=0)
    n_c = c_all.shape[0]
    c_all = jnp.pad(c_all, ((0, -n_c % 8), (0, 0)))
    new_p, new_s = [], []
    for l in range(DEPTH):
        lw = {"norm_mix": norm_mix[l], "norm_ffn": norm_ffn[l], "w_in": _pack_w_in(w_in[l]),
              "gdn_conv_w": gdn_conv_w[l], "gdn_a_log": gdn_a_log[l], "gdn_dt_bias": gdn_dt_bias[l],
              "gdn_norm_w": gdn_norm_w[l], "fox_f_bias": fox_f_bias[l], "ssm_conv_w": ssm_conv_w[l],
              "ssm_conv_b": ssm_conv_b[l], "ssm_a_log": ssm_a_log[l], "ssm_dt_bias": ssm_dt_bias[l],
              "ssm_d": ssm_d[l], "ssm_norm_w": ssm_norm_w[l], "w_branch": w_branch[l].astype(BF16),
              "w_out": w_out[l].astype(BF16), "peer_wq": peer_wq[l].astype(BF16),
              "peer_keys_t": peer_keys[l].transpose(0, 2, 1).astype(BF16),
              "peer_u": _pack_table(peer_u[l]), "peer_v": _pack_table(peer_v[l])}
        mod = _adaln(c_all, w_ada[l].astype(BF16), b_ada[l])
        mod_p = jnp.split(mod[:n_pr], 6, axis=-1)
        mod_s = jnp.split(mod[n_pr:n_c], 6, axis=-1)

        def fox_p(proj, p3, logf):
            return _fox_prompt(proj, jnp.cumsum(logf, axis=1), n_pr, seq)

        zero_states = (jnp.zeros((n_pr, GDN_HEADS, GDN_DK, GDN_DV), F32),
                       jnp.zeros((n_pr, CONV_W - 1, GDN_CONV_CH), F32),
                       jnp.zeros((n_pr, 4, 128, 128), F32),
                       jnp.zeros((n_pr, CONV_W - 1, SSM_CONV_CH), F32))
        xp, st = _layer(xp, mod_p, lw, zero_states, n_pr, seq, fox_p)
        new_p.append(st[:5] + (_ssm_state_unpack(st[5]), st[6]))

        ck = cache_k[l].reshape(n_pool, PAGE_SIZE, MIX_W)
        cv = cache_v[l].reshape(n_pool, PAGE_SIZE, MIX_W)
        clf = cache_logf[l].transpose(0, 2, 1)

        def fox_s(proj, p3, logf):
            qn = p3[:, :, C_BQKV:C_BQKV + 512]
            kn = _pad_seq(p3[:, :, C_BQKV + 512:C_BQKV + 1024], 1, PAGE_SIZE)
            vn = _pad_seq(p3[:, :, C_BQKV + 1024:C_BQKV + 1536], 1, PAGE_SIZE)
            lfn = _pad_seq(logf, 1, PAGE_SIZE).transpose(0, 2, 1)
            o = _fox_sample(qn, kn, vn, lfn, ck, cv, clf, page_table)
            return o.reshape(n_dec * dec_seq, MIX_W).astype(BF16)

        s_states = (state_gdn[l], state_gdn_conv[l], _ssm_state_pack(state_ssm[l]), state_ssm_conv[l])
        xs, st = _layer(xs, mod_s, lw, s_states, n_dec, dec_seq, fox_s)
        new_s.append(st[:5] + (_ssm_state_unpack(st[5]), st[6]))
    y_prompt = _final_norm(xp, final_norm).reshape(n_pr, seq, d)
    y_sample = _final_norm(xs, final_norm).reshape(n_dec, dec_seq, d)
    k_p, v_p, f_p, g_p, gc_p, s_p, sc_p = [jnp.stack(z) for z in zip(*new_p)]
    k_s, v_s, f_s, g_s, gc_s, s_s, sc_s = [jnp.stack(z) for z in zip(*new_s)]
    return (y_prompt, y_sample, k_p, v_p, f_p, g_p, gc_p, s_p, sc_p,
            k_s, v_s, f_s, g_s, gc_s, s_s, sc_s)
```

```python
import functools
import math

import jax
import jax.numpy as jnp
from jax import lax
from jax.experimental import pallas as pl
from jax.experimental.pallas import tpu as pltpu
from jax.experimental.pallas import tpu_sc as plsc

F32 = jnp.float32
BF16 = jnp.bfloat16

D_MODEL = 1024
DEPTH = 4
PAGE_SIZE = 128
MIX_W = D_MODEL // 2
CONV_W = 4
GDN_DK = 128
GDN_DV = 128
GDN_HEADS = MIX_W // GDN_DV
FOX_DH = 64
FOX_HEADS = MIX_W // FOX_DH
FOX_SCALE = FOX_DH ** -0.5
SSM_P = 64
SSM_HEADS = MIX_W // SSM_P
SSM_GROUPS = 2
SSM_N = 64
PEER_HEADS = 8
PEER_NKEYS = 128
PEER_DQ = 256
PEER_TOPK = 16
EPS = 1e-6

GDN_CONV_CH = 2 * GDN_HEADS * GDN_DK + GDN_HEADS * GDN_DV
SSM_CONV_CH = MIX_W + 2 * SSM_GROUPS * SSM_N
IN_SPLITS = (GDN_CONV_CH, MIX_W, GDN_HEADS, GDN_HEADS, 3 * MIX_W, FOX_HEADS,
             MIX_W, SSM_CONV_CH, SSM_HEADS, 3 * D_MODEL)

C_GATES = 0
C_AQKV = 3072
C_AGATE = 4608
C_BQKV = 5120
C_CZ = 6656
C_CXBC = 7168
C_SMALL = 7936
N_PROJ = 8192

LOG2E = 1.4426950408889634
CHUNK = 128
VMEM_LIMIT = 48 * 1024 * 1024


def _cparams(*sem):
    return pltpu.CompilerParams(dimension_semantics=sem, vmem_limit_bytes=VMEM_LIMIT)


def _pack_w_in(w_in):
    offs = [0]
    for s in IN_SPLITS:
        offs.append(offs[-1] + s)
    a_qkv, a_gate, a_beta, a_alpha, b_qkv, b_f, c_z, c_xbc, c_dt, gates = [
        w_in[:, offs[i]:offs[i + 1]] for i in range(len(IN_SPLITS))]
    small = jnp.concatenate([a_beta, a_alpha, b_f, c_dt], axis=1)
    small = jnp.pad(small, ((0, 0), (0, 128 - small.shape[1])))
    pad = jnp.zeros((w_in.shape[0], N_PROJ - C_SMALL - 128), w_in.dtype)
    out = jnp.concatenate([gates, a_qkv, a_gate, b_qkv, c_z, c_xbc, small, pad], axis=1)
    return out.astype(BF16)


def _ada_kernel(c_ref, w_ref, b_ref, o_ref):
    c = c_ref[...]
    a = (c * jax.nn.sigmoid(c)).astype(BF16)
    o_ref[...] = jnp.dot(a, w_ref[...], preferred_element_type=F32) + b_ref[...]


def _adaln(c, w, b):
    m, k = c.shape
    n = w.shape[1]
    tn = 512
    return pl.pallas_call(
        _ada_kernel,
        grid=(n // tn,),
        in_specs=[pl.BlockSpec((m, k), lambda j: (0, 0)),
                  pl.BlockSpec((k, tn), lambda j: (0, j)),
                  pl.BlockSpec((1, tn), lambda j: (0, j))],
        out_specs=pl.BlockSpec((m, tn), lambda j: (0, j)),
        out_shape=jax.ShapeDtypeStruct((m, n), F32),
        compiler_params=_cparams("parallel"),
        name="adaln",
    )(c, w, b.reshape(1, n))


def _mod_operand(m, seq_len, tm):
    bsz, d = m.shape
    if seq_len % tm == 0:
        per = seq_len // tm
        return m.reshape(bsz, 1, d), pl.BlockSpec((None, 1, d), lambda i, *_: (i // per, 0, 0))
    arr = jnp.repeat(m, seq_len, axis=0)
    return arr, pl.BlockSpec((tm, d), lambda i, *_: (i, 0))


def _rms_mod(x, nw, sc, sh):
    y = x * lax.rsqrt(jnp.mean(x * x, axis=-1, keepdims=True) + EPS)
    return (y * nw) * (1.0 + sc) + sh


def _proj_kernel(x_ref, nw_ref, sc_ref, sh_ref, w_ref, o_ref, h_scr):
    @pl.when(pl.program_id(1) == 0)
    def _():
        h_scr[...] = _rms_mod(x_ref[...], nw_ref[...], sc_ref[...], sh_ref[...]).astype(BF16)

    o_ref[...] = jnp.dot(h_scr[...], w_ref[...], preferred_element_type=F32)


def _proj_in(x, nw, sc, sh, w, seq_len):
    t, d = x.shape
    n = w.shape[1]
    tm = min(512, t)
    tn = 1024
    sc_a, sc_s = _mod_operand(sc, seq_len, tm)
    sh_a, sh_s = _mod_operand(sh, seq_len, tm)
    return pl.pallas_call(
        _proj_kernel,
        grid=(t // tm, n // tn),
        in_specs=[pl.BlockSpec((tm, d), lambda i, j: (i, 0)),
                  pl.BlockSpec((1, d), lambda i, j: (0, 0)),
                  sc_s, sh_s,
                  pl.BlockSpec((d, tn), lambda i, j: (0, j))],
        out_specs=pl.BlockSpec((tm, tn), lambda i, j: (i, j)),
        out_shape=jax.ShapeDtypeStruct((t, n), F32),
        scratch_shapes=[pltpu.VMEM((tm, d), BF16)],
        compiler_params=_cparams("parallel", "arbitrary"),
        name="proj_in",
    )(x, nw.reshape(1, d), sc_a, sh_a, w)


def _split(x, parts):
    out = []
    for _ in range(parts - 1):
        hi = x.astype(BF16)
        out.append(hi)
        x = x - hi.astype(F32)
    out.append(x.astype(BF16))
    return out


def _mxu(a, b, dims):
    return lax.dot_general(a, b, (dims, ((), ())), preferred_element_type=F32)


def _dot3(a, b, dims):
    ah, al = _split(a, 2)
    bh, bl = _split(b, 2)
    return _mxu(ah, bh, dims) + (_mxu(ah, bl, dims) + _mxu(al, bh, dims))


def _dot(a, b):
    return _dot3(a, b, ((1,), (0,)))


def _dot_nt(a, b):
    return _dot3(a, b, ((1,), (1,)))


def _mask_dot(mask, x):
    m = mask.astype(BF16)
    return sum(_mxu(m, p, ((1,), (0,))) for p in _split(x, 3))


def _dot_mask(x, mask):
    m = mask.astype(BF16)
    return sum(_mxu(p, m, ((1,), (0,))) for p in _split(x, 3))


def _cumsum_maps(g, c):
    row = lax.broadcasted_iota(jnp.int32, (c, c), 0)
    col = lax.broadcasted_iota(jnp.int32, (c, c), 1)
    tril = row >= col
    gb = jnp.broadcast_to(g, (c, c))
    ri = _mask_dot(tril, gb)
    rj = _mask_dot(jnp.ones((c, c), BF16), jnp.where(row <= col, gb, 0.0))
    return ri, rj, tril, row, col


def _gdn_head(q, k, v, g, beta, s, c, levels):
    ri, rj, tril, row, col = _cumsum_maps(g, c)
    decay = jnp.where(tril, jnp.exp(jnp.where(tril, ri - rj, 0.0)), 0.0)
    gc = ri[:, 0:1]
    gc_last = ri[c - 1:c, 0:1]
    kb = k * beta
    m = jnp.where(row > col, _dot_nt(kb, k) * decay, 0.0)
    eye = (row == col).astype(F32)
    blk = (row // 8) == (col // 8)
    p = jnp.where(blk, m, 0.0)
    x = eye - p
    for _ in range(levels[0]):
        p = _dot(p, p)
        x = x + _dot(x, p)
    b = 8
    for _ in range(levels[1]):
        low = jnp.where(((row // (2 * b)) == (col // (2 * b))) & ((row // b) != (col // b)), m, 0.0)
        x = x - _dot(_dot(x, low), x)
        b *= 2
    u_in = _dot(x, v * beta)
    wk_in = _dot(x, kb * jnp.exp(gc))
    qk = _dot_nt(q, k) * decay
    q_dec = q * jnp.exp(gc)
    k_dec = k * jnp.exp(gc_last - gc)
    u = u_in - _dot(wk_in, s)
    o = _dot(q_dec, s) + _dot(qk, u)
    s_new = s * jnp.exp(gc_last) + _dot(k_dec.T, u)
    return o, s_new


def _gdn_kernel(q_ref, k_ref, v_ref, gb_ref, gate_ref, nw_ref, s0_ref, o_ref, sout_ref, s_scr,
                *, c, levels):
    ci = pl.program_id(1)

    @pl.when(ci == 0)
    def _():
        s_scr[...] = s0_ref[...]

    for hd in range(GDN_HEADS):
        sl = slice(hd * GDN_DV, (hd + 1) * GDN_DV)
        o, s_new = _gdn_head(q_ref[:, sl], k_ref[:, sl], v_ref[:, sl], gb_ref[:, hd:hd + 1],
                             gb_ref[:, GDN_HEADS + hd:GDN_HEADS + hd + 1], s_scr[hd], c, levels)
        s_scr[hd] = s_new
        sout_ref[hd] = s_new
        y = o * lax.rsqrt(jnp.mean(o * o, axis=-1, keepdims=True) + EPS) * nw_ref[...]
        gt = gate_ref[:, sl]
        o_ref[:, sl] = (y * (gt * jax.nn.sigmoid(gt))).astype(o_ref.dtype)


def _gdn_scan(qkv, gb, gate, gate_col, nw, s0, n_valid):
    b, l, _ = qkv.shape
    c = CHUNK
    n_eff = min(n_valid, c)
    base = min(n_eff, 8)
    levels = (max(0, math.ceil(math.log2(base)) - 1) if base > 1 else 0,
              max(0, math.ceil(math.log2(n_eff / 8))) if n_eff > 8 else 0)
    col = lambda j: pl.BlockSpec((None, c, MIX_W), lambda bi, ci: (bi, ci, j))
    st = pl.BlockSpec((None, GDN_HEADS, GDN_DK, GDN_DV), lambda bi, ci: (bi, 0, 0, 0))
    return pl.pallas_call(
        functools.partial(_gdn_kernel, c=c, levels=levels),
        grid=(b, l // c),
        in_specs=[col(0), col(1), col(2),
                  pl.BlockSpec((None, c, 8), lambda bi, ci: (bi, ci, 0)),
                  col(gate_col),
                  pl.BlockSpec((1, GDN_DV), lambda bi, ci: (0, 0)),
                  st],
        out_specs=[col(0), st],
        out_shape=[jax.ShapeDtypeStruct((b, l, MIX_W), BF16),
                   jax.ShapeDtypeStruct((b, GDN_HEADS, GDN_DK, GDN_DV), F32)],
        scratch_shapes=[pltpu.VMEM((GDN_HEADS, GDN_DK, GDN_DV), F32)],
        compiler_params=_cparams("parallel", "arbitrary"),
        name="gdn_scan",
    )(qkv, qkv, qkv, gb, gate, nw.reshape(1, GDN_DV), s0)


def _ssd_kernel(x_ref, da_ref, z_ref, dskip_ref, nw_ref, h0_ref, y_ref, hout_ref, h_scr, *, c):
    ci = pl.program_id(1)

    @pl.when(ci == 0)
    def _():
        h_scr[...] = h0_ref[...]

    lane = lax.broadcasted_iota(jnp.int32, (c, 128), 1)
    lane_lo = lane < 64
    hrow = lax.broadcasted_iota(jnp.int32, (128, 128), 0) < 64
    ys = []
    for pi in range(4):
        gmask = (lane_lo if pi < 2 else jnp.logical_not(lane_lo)).astype(F32)
        x = x_ref[:, pi * 128:(pi + 1) * 128]
        bm = x_ref[:, MIX_W:MIX_W + 128] * gmask
        cm = x_ref[:, MIX_W + 128:MIX_W + 256] * gmask
        hst = h_scr[pi]
        cb = _dot_nt(cm, bm)
        y = jnp.zeros((c, 128), F32)
        h_new = jnp.zeros((128, 128), F32)
        for hh in range(2):
            hd = 2 * pi + hh
            dt = da_ref[:, hd:hd + 1]
            da = da_ref[:, 8 + hd:9 + hd]
            ri, rj, tril, _, _ = _cumsum_maps(da, c)
            seg = jnp.where(tril, jnp.exp(jnp.where(tril, ri - rj, 0.0)), 0.0)
            a = ri[:, 0:1]
            a_last = ri[c - 1:c, 0:1]
            hm = lane_lo if hh == 0 else jnp.logical_not(lane_lo)
            xdt = jnp.where(hm, x, 0.0) * dt
            y_h = _dot(cb * seg, xdt) + _dot_nt(cm * jnp.exp(a), hst)
            y = y + jnp.where(hm, y_h, 0.0)
            upd = _dot(xdt.T, bm * jnp.exp(a_last - a))
            hsel = hrow if hh == 0 else jnp.logical_not(hrow)
            h_new = h_new + jnp.where(hsel, hst * jnp.exp(a_last) + upd, 0.0)
        h_scr[pi] = h_new
        hout_ref[pi] = h_new
        ys.append(y)
    y = jnp.concatenate(ys, axis=1) + dskip_ref[...] * x_ref[:, 0:MIX_W]
    z = z_ref[...]
    yz = y * (z * jax.nn.sigmoid(z))
    y_ref[...] = (yz * lax.rsqrt(jnp.mean(yz * yz, axis=-1, keepdims=True) + EPS)
                  * nw_ref[...]).astype(y_ref.dtype)


def _ssd_scan(xbc, da, z, z_col, dskip, nw, h0):
    b, l, w = xbc.shape
    c = CHUNK
    st = pl.BlockSpec((None, 4, 128, 128), lambda bi, ci: (bi, 0, 0, 0))
    vec = pl.BlockSpec((1, MIX_W), lambda bi, ci: (0, 0))
    return pl.pallas_call(
        functools.partial(_ssd_kernel, c=c),
        grid=(b, l // c),
        in_specs=[pl.BlockSpec((None, c, w), lambda bi, ci: (bi, ci, 0)),
                  pl.BlockSpec((None, c, 16), lambda bi, ci: (bi, ci, 0)),
                  pl.BlockSpec((None, c, MIX_W), lambda bi, ci: (bi, ci, z_col)),
                  vec, vec, st],
        out_specs=[pl.BlockSpec((None, c, MIX_W), lambda bi, ci: (bi, ci, 0)), st],
        out_shape=[jax.ShapeDtypeStruct((b, l, MIX_W), BF16),
                   jax.ShapeDtypeStruct((b, 4, 128, 128), F32)],
        scratch_shapes=[pltpu.VMEM((4, 128, 128), F32)],
        compiler_params=_cparams("parallel", "arbitrary"),
        name="ssd_scan",
    )(xbc, da, z, dskip.reshape(1, MIX_W), nw.reshape(1, MIX_W), h0)


def _fox_prompt_kernel(q_ref, k_ref, v_ref, cq_ref, ck_ref, o_ref, kb_scr, vb_scr,
                       m_scr, l_scr, acc_scr, qh_scr, cq_scr, *, tq, tk):
    i = pl.program_id(2)

    @pl.when(i == 0)
    def _():
        kb_scr[...] = k_ref[...].astype(BF16)
        vb_scr[...] = v_ref[...].astype(BF16)

    lane = lax.broadcasted_iota(jnp.int32, (tq, 128), 1)
    q = q_ref[...] * (FOX_SCALE * LOG2E)
    rep = tk // 128
    for hh in range(2):
        hm = (lane < 64) if hh == 0 else (lane >= 64)
        qh_scr[hh] = jnp.where(hm, q, 0.0).astype(BF16)
        cq_scr[hh] = jnp.concatenate([jnp.broadcast_to(cq_ref[:, hh:hh + 1], (tq, 128))] * rep, axis=1)
        m_scr[hh] = jnp.full((tq, 128), -jnp.inf, F32)
        l_scr[hh] = jnp.zeros((tq, 128), F32)
        acc_scr[hh] = jnp.zeros((tq, 128), F32)

    def step(j, masked):
        off = pl.multiple_of(j * tk, tk)
        kk = kb_scr[pl.ds(off, tk), :]
        vv = vb_scr[pl.ds(off, tk), :]
        for hh in range(2):
            s = lax.dot_general(qh_scr[hh], kk, (((1,), (1,)), ((), ())),
                                preferred_element_type=F32)
            s = s + cq_scr[hh] - ck_ref[j, hh:hh + 1, :]
            if masked:
                rowp = lax.broadcasted_iota(jnp.int32, (tq, tk), 0)
                colp = lax.broadcasted_iota(jnp.int32, (tq, tk), 1)
                s = jnp.where(colp <= rowp, s, -jnp.inf)
            m_prev = m_scr[hh]
            m_new = jnp.maximum(m_prev, jnp.max(s, axis=1, keepdims=True))
            alpha = jnp.exp2(m_prev - m_new)
            p = jnp.exp2(s - jnp.concatenate([m_new] * rep, axis=1))
            l_scr[hh] = alpha * l_scr[hh] + jnp.sum(p, axis=1, keepdims=True)
            acc_scr[hh] = alpha * acc_scr[hh] + jnp.dot(p.astype(BF16), vv,
                                                        preferred_element_type=F32)
            m_scr[hh] = m_new

    def body(j, carry):
        step(j, False)
        return carry

    lax.fori_loop(0, i, body, 0)
    step(i, True)
    o_ref[...] = jnp.where(lane < 64, acc_scr[0] / l_scr[0], acc_scr[1] / l_scr[1]).astype(o_ref.dtype)


def _fox_prompt(proj, cum, bsz, seq):
    tq = tk = 512 if seq % 512 == 0 else seq
    nq = seq // tq
    cum = cum * LOG2E
    cq = cum.reshape(bsz, seq, 4, 2).transpose(0, 2, 1, 3)
    ck = cum.reshape(bsz, nq, tk, 4, 2).transpose(0, 3, 1, 4, 2)
    qb = C_BQKV // 128
    kb = (C_BQKV + 512) // 128
    vb = (C_BQKV + 1024) // 128
    return pl.pallas_call(
        functools.partial(_fox_prompt_kernel, tq=tq, tk=tk),
        grid=(bsz, 4, nq),
        in_specs=[pl.BlockSpec((tq, 128), lambda b, p, i: (b * nq + i, qb + p)),
                  pl.BlockSpec((seq, 128), lambda b, p, i: (b, kb + p)),
                  pl.BlockSpec((seq, 128), lambda b, p, i: (b, vb + p)),
                  pl.BlockSpec((None, None, tq, 2), lambda b, p, i: (b, p, i, 0)),
                  pl.BlockSpec((None, None, nq, 2, tk), lambda b, p, i: (b, p, 0, 0, 0))],
        out_specs=pl.BlockSpec((tq, 128), lambda b, p, i: (b * nq + i, p)),
        out_shape=jax.ShapeDtypeStruct((bsz * seq, 512), BF16),
        scratch_shapes=[pltpu.VMEM((seq, 128), BF16), pltpu.VMEM((seq, 128), BF16),
                        pltpu.VMEM((2, tq, 128), F32), pltpu.VMEM((2, tq, 128), F32),
                        pltpu.VMEM((2, tq, 128), F32), pltpu.VMEM((2, tq, 128), BF16),
                        pltpu.VMEM((2, tq, tk), F32)],
        compiler_params=_cparams("parallel", "parallel", "arbitrary"),
        name="fox_prompt",
    )(proj, proj, proj, cq, ck)


def _fox_sample_kernel(pt_ref, q_ref, kn_ref, vn_ref, csq_ref, csk_ref, *rest, n_new, pps):
    page_refs = [rest[3 * i:3 * i + 3] for i in range(pps)]
    o_ref, m_scr, l_scr, acc_scr, suf_scr = rest[3 * pps:]
    s_id = pl.program_id(1)
    n_steps = pl.num_programs(1)
    r = 8 * n_new
    q = (q_ref[...] * FOX_SCALE).astype(BF16)
    row = lax.broadcasted_iota(jnp.int32, (r, 128), 0)
    col = lax.broadcasted_iota(jnp.int32, (r, 128), 1)
    same_head = (col % 8) == (row % 8)

    def attend(parts):
        blocks = []
        for kx, _, fn in parts:
            s_all = lax.dot_general(q, kx.astype(BF16), (((1,), (1,)), ((), ())),
                                    preferred_element_type=F32)
            blocks.append([fn(s_all[:, c * 128:(c + 1) * 128], c) for c in range(kx.shape[0] // 128)])
        m_new = m_prev = m_scr[...]
        for blk in blocks:
            m_new = jnp.maximum(m_new, jnp.max(jnp.concatenate(blk, axis=1), axis=1, keepdims=True))
        alpha = jnp.exp(m_prev - m_new)
        l_new = alpha * l_scr[...]
        acc = alpha[:, 0:FOX_DH] * acc_scr[...]
        for (_, vx, _), blk in zip(parts, blocks):
            p = jnp.exp(jnp.concatenate(blk, axis=1) - jnp.concatenate([m_new] * len(blk), axis=1))
            l_new = l_new + jnp.sum(p, axis=1, keepdims=True)
            acc = acc + jnp.dot(p.astype(BF16), vx.astype(BF16), preferred_element_type=F32)
        l_scr[...] = l_new
        acc_scr[...] = acc
        m_scr[...] = m_new

    @pl.when(s_id == 0)
    def _():
        m_scr[...] = jnp.full((r, 128), -jnp.inf, F32)
        l_scr[...] = jnp.zeros((r, 128), F32)
        acc_scr[...] = jnp.zeros((r, FOX_DH), F32)
        suf_scr[...] = jnp.zeros((8, 128), F32)
        ok = same_head & (col < r) & (col // 8 <= row // 8)
        bias = csq_ref[...] - csk_ref[0:1, :]
        attend([(kn_ref[...], vn_ref[...], lambda sc, c: jnp.where(ok, sc + bias, -jnp.inf))])

    @pl.when(s_id > 0)
    def _():
        li = lax.broadcasted_iota(jnp.int32, (128, 128), 0)
        lj = lax.broadcasted_iota(jnp.int32, (128, 128), 1)
        res = (li % 8) == (lj % 8)
        cq = csq_ref[...]
        carry = suf_scr[0:1, :]
        parts = []
        for kp_ref, vp_ref, lp_ref in page_refs:
            t = lp_ref[...]
            within = _dot_mask(t, res & (li > lj))
            tot = _dot_mask(t, res)
            rows = [tot[i:i + 1, :] for i in range(8)]
            later = [jnp.zeros((1, 128), F32)] * 8
            for i in range(6, -1, -1):
                later[i] = later[i + 1] + rows[i + 1]
            suf = within + jnp.concatenate(later, axis=0) + carry
            parts.append((kp_ref[...].reshape(PAGE_SIZE * 8, FOX_DH),
                          vp_ref[...].reshape(PAGE_SIZE * 8, FOX_DH),
                          lambda sc, c, suf=suf: jnp.where(same_head, sc + cq + suf[c:c + 1, :], -jnp.inf)))
            carry = carry + later[0] + rows[0]
        attend(parts)
        suf_scr[...] = jnp.broadcast_to(carry, (8, 128))

    @pl.when(s_id == n_steps - 1)
    def _():
        o_ref[...] = acc_scr[...] / l_scr[:, 0:FOX_DH]


def _fox_sample(q, k_new, v_new, logf_new, cache_k, cache_v, cache_lf, page_table, layer):
    bd, n_new, _ = q.shape
    n_pages = page_table.shape[1]
    r = 8 * n_new
    flat = lambda a: a.reshape(bd, r, FOX_DH)
    padk = lambda a: jnp.pad(flat(a), ((0, 0), (0, 128 - r), (0, 0)))
    cs = jnp.cumsum(logf_new, axis=1).reshape(bd, r)
    csq = jnp.broadcast_to(cs[:, :, None], (bd, r, 128))
    csk = jnp.pad(cs, ((0, 0), (0, 128 - r)))[:, None, :]
    csk = jnp.pad(csk, ((0, 0), (0, 7), (0, 0)))

    pps = 2 if n_pages % 2 == 0 else 1

    def page(i):
        return lambda b, s, pt: pt[b, n_pages - 1 - ((jnp.maximum(s, 1) - 1) * pps + i)]

    per_b = lambda shape: pl.BlockSpec((None,) + shape, lambda b, s, pt: (b, 0, 0))
    page_specs, page_args = [], []
    for i in range(pps):
        pg = page(i)
        kv = pl.BlockSpec((None, None, PAGE_SIZE, FOX_HEADS, FOX_DH),
                          lambda b, s, pt, pg=pg: (layer, pg(b, s, pt), 0, 0, 0))
        lf = pl.BlockSpec((None, None, 8, 128), lambda b, s, pt, pg=pg: (layer, pg(b, s, pt), 0, 0))
        page_specs += [kv, kv, lf]
        page_args += [cache_k, cache_v, cache_lf]
    grid_spec = pltpu.PrefetchScalarGridSpec(
        num_scalar_prefetch=1,
        grid=(bd, n_pages // pps + 1),
        in_specs=[per_b((r, FOX_DH)), per_b((128, FOX_DH)), per_b((128, FOX_DH)),
                  per_b((r, 128)), per_b((8, 128))] + page_specs,
        out_specs=per_b((r, FOX_DH)),
        scratch_shapes=[pltpu.VMEM((r, 128), F32), pltpu.VMEM((r, 128), F32),
                        pltpu.VMEM((r, FOX_DH), F32), pltpu.VMEM((8, 128), F32)])
    o = pl.pallas_call(
        functools.partial(_fox_sample_kernel, n_new=n_new, pps=pps),
        grid_spec=grid_spec,
        out_shape=jax.ShapeDtypeStruct((bd, r, FOX_DH), F32),
        compiler_params=_cparams("parallel", "arbitrary"),
        name="fox_sample",
    )(page_table, flat(q), padk(k_new), padk(v_new), csq, csk, *page_args)
    return o.reshape(bd * n_new, MIX_W)


def _merge_kernel(oa_ref, ob_ref, oc_ref, g0_ref, g1_ref, g2_ref, wb_ref, wo_ref, x_ref,
                  gate_ref, nw_ref, sc_ref, sh_ref, xo_ref, h_ref):
    acc = None
    for br, (o_r, g_r) in enumerate(((oa_ref, g0_ref), (ob_ref, g1_ref), (oc_ref, g2_ref))):
        y = jnp.dot(o_r[...].astype(BF16), wb_ref[br], preferred_element_type=F32)
        y = jax.nn.sigmoid(g_r[...]) * y
        acc = y if acc is None else acc + y
    out = jnp.dot(acc.astype(BF16), wo_ref[...], preferred_element_type=F32)
    xn = x_ref[...] + gate_ref[...] * out
    xo_ref[...] = xn
    h_ref[...] = _rms_mod(xn, nw_ref[...], sc_ref[...], sh_ref[...])


def _merge(o_a, o_b, o_c, proj, wb, wo, x, g1, nw, sc2, sh2, seq_len):
    t, d = x.shape
    tm = min(256, t)
    g_a, g_s = _mod_operand(g1, seq_len, tm)
    sc_a, sc_s = _mod_operand(sc2, seq_len, tm)
    sh_a, sh_s = _mod_operand(sh2, seq_len, tm)
    br = pl.BlockSpec((tm, MIX_W), lambda i: (i, 0))
    row = pl.BlockSpec((tm, d), lambda i: (i, 0))
    return pl.pallas_call(
        _merge_kernel,
        grid=(t // tm,),
        in_specs=[br, br, br,
                  pl.BlockSpec((tm, d), lambda i: (i, 0)),
                  pl.BlockSpec((tm, d), lambda i: (i, 1)),
                  pl.BlockSpec((tm, d), lambda i: (i, 2)),
                  pl.BlockSpec((3, MIX_W, d), lambda i: (0, 0, 0)),
                  pl.BlockSpec((d, d), lambda i: (0, 0)),
                  row, g_s,
                  pl.BlockSpec((1, d), lambda i: (0, 0)),
                  sc_s, sh_s],
        out_specs=[row, row],
        out_shape=[jax.ShapeDtypeStruct((t, d), F32), jax.ShapeDtypeStruct((t, d), F32)],
        compiler_params=_cparams("parallel"),
        name="merge",
    )(o_a, o_b, o_c, proj, proj, proj, wb, wo, x, g_a, nw.reshape(1, d), sc_a, sh_a)


def _top16_rows(s):
    rows, tb = s.shape
    rid = lax.broadcasted_iota(jnp.int32, (rows, tb), 0).astype(F32)
    out_row = lax.broadcasted_iota(jnp.int32, (PEER_TOPK, tb), 0)
    vals = jnp.zeros((PEER_TOPK, tb), F32)
    idxs = jnp.zeros((PEER_TOPK, tb), F32)
    for it in range(PEER_TOPK):
        mx = jnp.max(s, axis=0, keepdims=True)
        ix = jnp.min(jnp.where(s == mx, rid, float(rows)), axis=0, keepdims=True)
        vals = jnp.where(out_row == it, mx, vals)
        idxs = jnp.where(out_row == it, ix, idxs)
        s = jnp.where(rid == ix, -jnp.inf, s)
    return vals, idxs


_CAND_GROUPS = (("i", 0, 0, 16), ("j", 0, 1, 16), ("i", 1, 1, 8), ("j", 1, 2, 8),
                ("i", 2, 2, 5), ("i", 3, 2, 4), ("i", 4, 2, 3))


def _candidates(a, b, ia, ib):
    tb = a.shape[1]
    vals, poss, cids = [], [], []
    for kind, fixed, lo, hi in _CAND_GROUPS:
        n = 16 if hi > 8 else 8
        r = lax.broadcasted_iota(jnp.int32, (n, tb), 0)
        ok = (r >= lo) & (r < hi)
        if kind == "i":
            v = a[fixed:fixed + 1] + b[0:n]
            c = ia[fixed:fixed + 1] * float(PEER_NKEYS) + ib[0:n]
            p = fixed * PEER_TOPK + r
        else:
            v = a[0:n] + b[fixed:fixed + 1]
            c = ia[0:n] * float(PEER_NKEYS) + ib[fixed:fixed + 1]
            p = r * PEER_TOPK + fixed
        vals.append(jnp.where(ok, v, -jnp.inf))
        poss.append(jnp.where(ok, p, 4096).astype(F32))
        cids.append(c)
    return (jnp.concatenate(vals, axis=0), jnp.concatenate(poss, axis=0),
            jnp.concatenate(cids, axis=0))


def _route_kernel(h_ref, wq_ref, keys_ref, eid_ref, gate_ref, *, tb):
    qt = lax.dot_general(wq_ref[...], h_ref[...].astype(BF16), (((1,), (1,)), ((), ())),
                         preferred_element_type=F32)
    out_row = lax.broadcasted_iota(jnp.int32, (PEER_TOPK, tb), 0)
    tops, eids = [], []
    for hd in range(PEER_HEADS):
        sv, si = [], []
        for half in range(2):
            c0 = (hd * 2 + half) * 128
            sc = jnp.dot(keys_ref[half], qt[c0:c0 + 128, :].astype(BF16),
                         preferred_element_type=F32)
            v_, i_ = _top16_rows(sc)
            sv.append(v_)
            si.append(i_)
        cand, cpos, cid = _candidates(sv[0], sv[1], si[0], si[1])
        top = jnp.zeros((PEER_TOPK, tb), F32)
        eid = jnp.zeros((PEER_TOPK, tb), F32)
        for it in range(PEER_TOPK):
            mx = jnp.max(cand, axis=0, keepdims=True)
            px = jnp.min(jnp.where(cand == mx, cpos, 8192.0), axis=0, keepdims=True)
            hit = cpos == px
            ev = jnp.sum(jnp.where(hit, cid, 0.0), axis=0, keepdims=True)
            top = jnp.where(out_row == it, mx, top)
            eid = jnp.where(out_row == it, ev, eid)
            cand = jnp.where(hit, -jnp.inf, cand)
        e = jnp.exp(top - top[0:1])
        tops.append(e / jnp.sum(e, axis=0, keepdims=True))
        eids.append(eid)
    gate_ref[...] = jnp.concatenate(tops, axis=0)
    eid_ref[...] = jnp.concatenate(eids, axis=0).T.astype(jnp.int32)


def _route(h, wq_t, keys):
    t, d = h.shape
    tb = 128
    return pl.pallas_call(
        functools.partial(_route_kernel, tb=tb),
        grid=(t // tb,),
        in_specs=[pl.BlockSpec((tb, d), lambda i: (i, 0)),
                  pl.BlockSpec((PEER_HEADS * PEER_DQ, d), lambda i: (0, 0)),
                  pl.BlockSpec((2, 128, 128), lambda i: (0, 0, 0))],
        out_specs=[pl.BlockSpec((tb, 128), lambda i: (i, 0)),
                   pl.BlockSpec((None, 128, tb), lambda i: (i, 0, 0))],
        out_shape=[jax.ShapeDtypeStruct((t, 128), jnp.int32),
                   jax.ShapeDtypeStruct((t // tb, 128, tb), F32)],
        compiler_params=_cparams("parallel"),
        name="peer_route",
    )(h, wq_t, keys)


GATHER_WINDOW = 64


def _sc_gather(table, idx):
    n = idx.shape[0]
    width = table.shape[1]
    mesh = plsc.VectorSubcoreMesh(core_axis_name="core", subcore_axis_name="subcore")
    n_workers = mesh.num_cores * mesh.num_subcores
    win = GATHER_WINDOW
    per_w = n // n_workers
    n_chunks = per_w // win
    assert per_w * n_workers == n and n_chunks * win == per_w and n_chunks % 2 == 0

    @pl.kernel(out_type=jax.ShapeDtypeStruct((n, width), table.dtype), mesh=mesh,
               scratch_types=[pltpu.VMEM((win,), jnp.int32), pltpu.VMEM((win,), jnp.int32),
                              pltpu.VMEM((win, width), table.dtype),
                              pltpu.VMEM((win, width), table.dtype),
                              pltpu.SemaphoreType.DMA, pltpu.SemaphoreType.DMA,
                              pltpu.SemaphoreType.DMA, pltpu.SemaphoreType.DMA],
               name="peer_gather")
    def gather_kernel(tab_hbm, idx_hbm, out_hbm, i0, i1, r0, r1, g0, g1, w0, w1):
        wid = lax.axis_index("subcore") * mesh.num_cores + lax.axis_index("core")
        base = wid * per_w
        bufs = ((i0, r0, g0, w0), (i1, r1, g1, w1))

        def write_copy(rv, ws, off):
            return pltpu.make_async_copy(rv, out_hbm.at[pl.ds(off, win)], ws)

        @pl.loop(0, n_chunks, step=2)
        def _(c):
            for b, (iv, rv, gs, ws) in enumerate(bufs):
                off = base + (c + b) * win

                @pl.when(c > 0)
                def _():
                    write_copy(rv, ws, off).wait()

                pltpu.sync_copy(idx_hbm.at[pl.ds(off, win)], iv)
                pltpu.make_async_copy(tab_hbm.at[iv], rv, gs).start()
            for b, (iv, rv, gs, ws) in enumerate(bufs):
                off = base + (c + b) * win
                pltpu.make_async_copy(tab_hbm.at[iv], rv, gs).wait()
                write_copy(rv, ws, off).start()

        for iv, rv, gs, ws in bufs:
            write_copy(rv, ws, base).wait()

    return gather_kernel(table, idx)


def _pack_table(tab):
    b = lax.bitcast_convert_type(tab.astype(BF16), jnp.uint16).astype(jnp.uint32)
    return lax.bitcast_convert_type(b[:, :512] | (b[:, 512:] << 16), jnp.int32)


def _unpack(w):
    lo = pltpu.bitcast(w << 16, F32)
    hi = pltpu.bitcast(w & jnp.int32(-65536), F32)
    return lo, hi


def _peer_kernel(gu_ref, gv_ref, h_ref, gt_ref, x_ref, g2_ref, o_ref, *, tb):
    base = (pl.program_id(0) * tb) % 128
    gt = gt_ref[...]
    tok = lax.broadcasted_iota(jnp.int32, (128, 128), 1)
    for t in range(tb):
        h_lo = h_ref[t:t + 1, 0:512]
        h_hi = h_ref[t:t + 1, 512:1024]
        u_lo, u_hi = _unpack(gu_ref[t * 128:(t + 1) * 128, :])
        pu = u_lo * h_lo + u_hi * h_hi
        p128 = pu[:, 0:128] + pu[:, 128:256] + pu[:, 256:384] + pu[:, 384:512]
        pre = jnp.sum(p128, axis=1, keepdims=True)
        act = 0.5 * pre * (1.0 + lax.erf(pre * (2.0 ** -0.5)))
        gcol = jnp.sum(jnp.where(tok == base + t, gt, 0.0), axis=1, keepdims=True)
        w = gcol * act
        v_lo, v_hi = _unpack(gv_ref[t * 128:(t + 1) * 128, :])
        y_lo = jnp.sum(v_lo * w, axis=0, keepdims=True)
        y_hi = jnp.sum(v_hi * w, axis=0, keepdims=True)
        o_ref[t:t + 1, 0:512] = x_ref[t:t + 1, 0:512] + g2_ref[t:t + 1, 0:512] * y_lo
        o_ref[t:t + 1, 512:1024] = x_ref[t:t + 1, 512:1024] + g2_ref[t:t + 1, 512:1024] * y_hi


def _peer_mix(gu, gv, h, gate_t, x, g2, seq_len):
    t, d = x.shape
    tb = 16
    if seq_len % tb == 0:
        per = seq_len // tb
        g_a = jnp.broadcast_to(g2.reshape(g2.shape[0], 1, d), (g2.shape[0], tb, d))
        g_s = pl.BlockSpec((None, tb, d), lambda i: (i // per, 0, 0))
    else:
        g_a = jnp.repeat(g2, seq_len, axis=0)
        g_s = pl.BlockSpec((tb, d), lambda i: (i, 0))
    row = pl.BlockSpec((tb, d), lambda i: (i, 0))
    rows = pl.BlockSpec((tb * 128, 512), lambda i: (i, 0))
    per_grp = 128 // tb
    return pl.pallas_call(
        functools.partial(_peer_kernel, tb=tb),
        grid=(t // tb,),
        in_specs=[rows, rows, row,
                  pl.BlockSpec((None, 128, 128), lambda i: (i // per_grp, 0, 0)),
                  row, g_s],
        out_specs=row,
        out_shape=jax.ShapeDtypeStruct((t, d), F32),
        compiler_params=_cparams("parallel"),
        name="peer_mix",
    )(gu, gv, h, gate_t, x, g_a)


def _final_norm_kernel(x_ref, w_ref, o_ref):
    x = x_ref[...]
    o_ref[...] = x * lax.rsqrt(jnp.mean(x * x, axis=-1, keepdims=True) + EPS) * w_ref[...]


def _final_norm(x, w):
    t, d = x.shape
    tm = min(512, t)
    return pl.pallas_call(
        _final_norm_kernel,
        grid=(t // tm,),
        in_specs=[pl.BlockSpec((tm, d), lambda i: (i, 0)), pl.BlockSpec((1, d), lambda i: (0, 0))],
        out_specs=pl.BlockSpec((tm, d), lambda i: (i, 0)),
        out_shape=jax.ShapeDtypeStruct((t, d), F32),
        compiler_params=_cparams("parallel"),
        name="final_norm",
    )(x, w.reshape(1, d))


def _causal_conv(x, buf, w):
    l = x.shape[1]
    xp = jnp.concatenate([buf.astype(x.dtype), x], axis=1)
    y = sum(xp[:, i:i + l] * w[i] for i in range(CONV_W))
    return y, xp[:, l:]


def _l2norm(x):
    return x * lax.rsqrt(jnp.sum(x * x, axis=-1, keepdims=True) + EPS)


def _pad_seq(a, axis, lp):
    pad = lp - a.shape[axis]
    if pad == 0:
        return a
    cfg = [(0, 0)] * a.ndim
    cfg[axis] = (0, pad)
    return jnp.pad(a, cfg)


def _layer(x, mod, lw, states, bsz, seq, fox):
    sh1, sc1, g1, sh2, sc2, g2 = mod
    gdn_s, gdn_b, ssm_s, ssm_b = states
    t = bsz * seq
    lp = -(-seq // CHUNK) * CHUNK
    proj = _proj_in(x, lw["norm_mix"], sc1, sh1, lw["w_in"], seq)
    p3 = proj.reshape(bsz, seq, N_PROJ)
    small = p3[:, :, C_SMALL:C_SMALL + 24]
    a_beta, a_alpha, b_f, c_dt = small[..., 0:4], small[..., 4:8], small[..., 8:16], small[..., 16:24]

    a_conv, gdn_b_new = _causal_conv(p3[:, :, C_AQKV:C_AQKV + GDN_CONV_CH], gdn_b, lw["gdn_conv_w"])
    a_act = jax.nn.silu(a_conv)
    q = _l2norm(a_act[..., 0:512].reshape(bsz, seq, GDN_HEADS, GDN_DK)) * GDN_DK ** -0.5
    k = _l2norm(a_act[..., 512:1024].reshape(bsz, seq, GDN_HEADS, GDN_DK))
    qkv = jnp.concatenate([q.reshape(bsz, seq, MIX_W), k.reshape(bsz, seq, MIX_W),
                           a_act[..., 1024:1536]], axis=-1)
    beta = jax.nn.sigmoid(a_beta)
    g = -jnp.exp(lw["gdn_a_log"]) * jax.nn.softplus(a_alpha + lw["gdn_dt_bias"])
    gb = jnp.concatenate([g, beta], axis=-1)
    if seq % CHUNK == 0:
        gate_arr, gate_col = p3, C_AGATE // MIX_W
    else:
        gate_arr, gate_col = _pad_seq(p3[:, :, C_AGATE:C_AGATE + MIX_W], 1, lp), 0
    o_a, gdn_s_new = _gdn_scan(_pad_seq(qkv, 1, lp), _pad_seq(gb, 1, lp), gate_arr, gate_col,
                               lw["gdn_norm_w"], gdn_s, seq)
    o_a = o_a[:, :seq].reshape(t, MIX_W)

    logf = jax.nn.log_sigmoid(b_f + lw["fox_f_bias"])
    kf = p3[:, :, C_BQKV + 512:C_BQKV + 1024].reshape(bsz, seq, FOX_HEADS, FOX_DH)
    vf = p3[:, :, C_BQKV + 1024:C_BQKV + 1536].reshape(bsz, seq, FOX_HEADS, FOX_DH)
    o_b = fox(proj, p3, logf)

    xbc, ssm_b_new = _causal_conv(p3[:, :, C_CXBC:C_CXBC + SSM_CONV_CH], ssm_b, lw["ssm_conv_w"])
    xbc = jax.nn.silu(xbc + lw["ssm_conv_b"])
    dt = jax.nn.softplus(c_dt + lw["ssm_dt_bias"])
    da = jnp.concatenate([dt, dt * (-jnp.exp(lw["ssm_a_log"]))], axis=-1)
    if seq % CHUNK == 0:
        z_arr, z_col = p3, C_CZ // MIX_W
    else:
        z_arr, z_col = _pad_seq(p3[:, :, C_CZ:C_CZ + MIX_W], 1, lp), 0
    o_c, ssm_s_new = _ssd_scan(_pad_seq(xbc, 1, lp), _pad_seq(da, 1, lp), z_arr, z_col,
                               jnp.repeat(lw["ssm_d"], SSM_P), lw["ssm_norm_w"], ssm_s)
    o_c = o_c[:, :seq].reshape(t, MIX_W)

    x, h2 = _merge(o_a, o_b, o_c, proj, lw["w_branch"], lw["w_out"], x, g1, lw["norm_ffn"],
                   sc2, sh2, seq)
    eid, gate_t = _route(h2, lw["peer_wq_t"], lw["peer_keys"])
    flat = eid.reshape(-1)
    gu = _sc_gather(lw["peer_u"], flat)
    gv = _sc_gather(lw["peer_v"], flat)
    x = _peer_mix(gu, gv, h2, gate_t, x, g2, seq)
    return x, (kf, vf, logf, gdn_s_new, gdn_b_new, ssm_s_new, ssm_b_new)


def _ssm_state_pack(s):
    b = s.shape[0]
    sp = s.reshape(b, 4, 128, SSM_N)
    z = jnp.zeros_like(sp)
    lo = jnp.concatenate([sp, z], axis=-1)
    hi = jnp.concatenate([z, sp], axis=-1)
    grp = (jnp.arange(4) // 2).reshape(1, 4, 1, 1)
    return jnp.where(grp == 0, lo, hi)


def _ssm_state_unpack(sp):
    b = sp.shape[0]
    out = jnp.where((jnp.arange(4) // 2).reshape(1, 4, 1, 1) == 0, sp[..., :SSM_N], sp[..., SSM_N:])
    return out.reshape(b, SSM_HEADS, SSM_P, SSM_N)


def kernel(x_prompt, x_sample, cache_k, cache_v, cache_logf, page_table, state_gdn, state_gdn_conv,
           state_ssm, state_ssm_conv, c_prompt, c_sample, w_ada, b_ada, norm_mix, norm_ffn, w_in,
           gdn_conv_w, gdn_a_log, gdn_dt_bias, gdn_norm_w, fox_f_bias, ssm_conv_w, ssm_conv_b,
           ssm_a_log, ssm_dt_bias, ssm_d, ssm_norm_w, w_branch, w_out, peer_wq, peer_keys, peer_u,
           peer_v, final_norm):
    n_pr, seq, d = x_prompt.shape
    n_dec, dec_seq, _ = x_sample.shape
    n_pool = cache_k.shape[1]
    xp = x_prompt.reshape(n_pr * seq, d)
    xs = x_sample.reshape(n_dec * dec_seq, d)
    c_all = jnp.concatenate([c_prompt, c_sample], axis=0)
    n_c = c_all.shape[0]
    c_all = jnp.pad(c_all, ((0, -n_c % 8), (0, 0)))
    cache_lf = cache_logf.reshape(DEPTH, n_pool, 8, PAGE_SIZE * FOX_HEADS // 8)
    new_p, new_s = [], []
    for l in range(DEPTH):
        lw = {"norm_mix": norm_mix[l], "norm_ffn": norm_ffn[l], "w_in": _pack_w_in(w_in[l]),
              "gdn_conv_w": gdn_conv_w[l], "gdn_a_log": gdn_a_log[l], "gdn_dt_bias": gdn_dt_bias[l],
              "gdn_norm_w": gdn_norm_w[l], "fox_f_bias": fox_f_bias[l], "ssm_conv_w": ssm_conv_w[l],
              "ssm_conv_b": ssm_conv_b[l], "ssm_a_log": ssm_a_log[l], "ssm_dt_bias": ssm_dt_bias[l],
              "ssm_d": ssm_d[l], "ssm_norm_w": ssm_norm_w[l], "w_branch": w_branch[l].astype(BF16),
              "w_out": w_out[l].astype(BF16), "peer_wq_t": peer_wq[l].T.astype(BF16),
              "peer_keys": peer_keys[l].astype(BF16),
              "peer_u": _pack_table(peer_u[l]), "peer_v": _pack_table(peer_v[l])}
        mod = _adaln(c_all, w_ada[l].astype(BF16), b_ada[l])
        mod_p = jnp.split(mod[:n_pr], 6, axis=-1)
        mod_s = jnp.split(mod[n_pr:n_c], 6, axis=-1)

        def fox_p(proj, p3, logf):
            return _fox_prompt(proj, jnp.cumsum(logf, axis=1), n_pr, seq)

        zero_states = (jnp.zeros((n_pr, GDN_HEADS, GDN_DK, GDN_DV), F32),
                       jnp.zeros((n_pr, CONV_W - 1, GDN_CONV_CH), F32),
                       jnp.zeros((n_pr, 4, 128, 128), F32),
                       jnp.zeros((n_pr, CONV_W - 1, SSM_CONV_CH), F32))
        xp, st = _layer(xp, mod_p, lw, zero_states, n_pr, seq, fox_p)
        new_p.append(st[:5] + (_ssm_state_unpack(st[5]), st[6]))

        def fox_s(proj, p3, logf, l=l):
            qkv = [p3[:, :, C_BQKV + i * MIX_W:C_BQKV + (i + 1) * MIX_W] for i in range(3)]
            return _fox_sample(*qkv, logf, cache_k, cache_v, cache_lf, page_table, l).astype(BF16)

        s_states = (state_gdn[l], state_gdn_conv[l], _ssm_state_pack(state_ssm[l]), state_ssm_conv[l])
        xs, st = _layer(xs, mod_s, lw, s_states, n_dec, dec_seq, fox_s)
        new_s.append(st[:5] + (_ssm_state_unpack(st[5]), st[6]))
    y_prompt = _final_norm(xp, final_norm).reshape(n_pr, seq, d)
    y_sample = _final_norm(xs, final_norm).reshape(n_dec, dec_seq, d)
    k_p, v_p, f_p, g_p, gc_p, s_p, sc_p = [jnp.stack(z) for z in zip(*new_p)]
    k_s, v_s, f_s, g_s, gc_s, s_s, sc_s = [jnp.stack(z) for z in zip(*new_s)]
    return (y_prompt, y_sample, k_p, v_p, f_p, g_p, gc_p, s_p, sc_p,
            k_s, v_s, f_s, g_s, gc_s, s_s, sc_s)
```

```python
import functools
import math

import jax
import jax.numpy as jnp
from jax import lax
from jax.experimental import pallas as pl
from jax.experimental.pallas import tpu as pltpu
from jax.experimental.pallas import tpu_sc as plsc

F32 = jnp.float32
BF16 = jnp.bfloat16

D_MODEL = 1024
DEPTH = 4
PAGE_SIZE = 128
MIX_W = D_MODEL // 2
CONV_W = 4
GDN_DK = 128
GDN_DV = 128
GDN_HEADS = MIX_W // GDN_DV
FOX_DH = 64
FOX_HEADS = MIX_W // FOX_DH
FOX_SCALE = FOX_DH ** -0.5
SSM_P = 64
SSM_HEADS = MIX_W // SSM_P
SSM_GROUPS = 2
SSM_N = 64
PEER_HEADS = 8
PEER_NKEYS = 128
PEER_DQ = 256
PEER_TOPK = 16
EPS = 1e-6

GDN_CONV_CH = 2 * GDN_HEADS * GDN_DK + GDN_HEADS * GDN_DV
SSM_CONV_CH = MIX_W + 2 * SSM_GROUPS * SSM_N
IN_SPLITS = (GDN_CONV_CH, MIX_W, GDN_HEADS, GDN_HEADS, 3 * MIX_W, FOX_HEADS,
             MIX_W, SSM_CONV_CH, SSM_HEADS, 3 * D_MODEL)

C_GATES = 0
C_AQKV = 3072
C_AGATE = 4608
C_BQKV = 5120
C_CZ = 6656
C_CXBC = 7168
C_SMALL = 7936
N_PROJ = 8192

LOG2E = 1.4426950408889634
CHUNK = 128
VMEM_LIMIT = 48 * 1024 * 1024


def _cparams(*sem):
    return pltpu.CompilerParams(dimension_semantics=sem, vmem_limit_bytes=VMEM_LIMIT)


def _pack_w_in(w_in):
    offs = [0]
    for s in IN_SPLITS:
        offs.append(offs[-1] + s)
    a_qkv, a_gate, a_beta, a_alpha, b_qkv, b_f, c_z, c_xbc, c_dt, gates = [
        w_in[:, offs[i]:offs[i + 1]] for i in range(len(IN_SPLITS))]
    small = jnp.concatenate([a_beta, a_alpha, b_f, c_dt], axis=1)
    small = jnp.pad(small, ((0, 0), (0, 128 - small.shape[1])))
    pad = jnp.zeros((w_in.shape[0], N_PROJ - C_SMALL - 128), w_in.dtype)
    out = jnp.concatenate([gates, a_qkv, a_gate, b_qkv, c_z, c_xbc, small, pad], axis=1)
    return out.astype(BF16)


def _ada_kernel(c_ref, w_ref, b_ref, o_ref):
    c = c_ref[...]
    a = (c * jax.nn.sigmoid(c)).astype(BF16)
    o_ref[...] = jnp.dot(a, w_ref[...], preferred_element_type=F32) + b_ref[...]


def _adaln(c, w, b):
    m, k = c.shape
    n = w.shape[1]
    tn = 512
    return pl.pallas_call(
        _ada_kernel,
        grid=(n // tn,),
        in_specs=[pl.BlockSpec((m, k), lambda j: (0, 0)),
                  pl.BlockSpec((k, tn), lambda j: (0, j)),
                  pl.BlockSpec((1, tn), lambda j: (0, j))],
        out_specs=pl.BlockSpec((m, tn), lambda j: (0, j)),
        out_shape=jax.ShapeDtypeStruct((m, n), F32),
        compiler_params=_cparams("parallel"),
        name="adaln",
    )(c, w, b.reshape(1, n))


def _mod_operand(m, seq_len, tm):
    bsz, d = m.shape
    if seq_len % tm == 0:
        per = seq_len // tm
        return m.reshape(bsz, 1, d), pl.BlockSpec((None, 1, d), lambda i, *_: (i // per, 0, 0))
    arr = jnp.repeat(m, seq_len, axis=0)
    return arr, pl.BlockSpec((tm, d), lambda i, *_: (i, 0))


def _rms_mod(x, nw, sc, sh):
    y = x * lax.rsqrt(jnp.mean(x * x, axis=-1, keepdims=True) + EPS)
    return (y * nw) * (1.0 + sc) + sh


def _proj_kernel(x_ref, nw_ref, sc_ref, sh_ref, w_ref, o_ref, h_scr):
    @pl.when(pl.program_id(1) == 0)
    def _():
        h_scr[...] = _rms_mod(x_ref[...], nw_ref[...], sc_ref[...], sh_ref[...]).astype(BF16)

    o_ref[...] = jnp.dot(h_scr[...], w_ref[...], preferred_element_type=F32)


def _proj_in(x, nw, sc, sh, w, seq_len):
    t, d = x.shape
    n = w.shape[1]
    tm = min(512, t)
    tn = 1024
    sc_a, sc_s = _mod_operand(sc, seq_len, tm)
    sh_a, sh_s = _mod_operand(sh, seq_len, tm)
    return pl.pallas_call(
        _proj_kernel,
        grid=(t // tm, n // tn),
        in_specs=[pl.BlockSpec((tm, d), lambda i, j: (i, 0)),
                  pl.BlockSpec((1, d), lambda i, j: (0, 0)),
                  sc_s, sh_s,
                  pl.BlockSpec((d, tn), lambda i, j: (0, j))],
        out_specs=pl.BlockSpec((tm, tn), lambda i, j: (i, j)),
        out_shape=jax.ShapeDtypeStruct((t, n), F32),
        scratch_shapes=[pltpu.VMEM((tm, d), BF16)],
        compiler_params=_cparams("parallel", "arbitrary"),
        name="proj_in",
    )(x, nw.reshape(1, d), sc_a, sh_a, w)


def _split(x, parts):
    out = []
    for _ in range(parts - 1):
        hi = x.astype(BF16)
        out.append(hi)
        x = x - hi.astype(F32)
    out.append(x.astype(BF16))
    return out


def _mxu(a, b, dims):
    return lax.dot_general(a, b, (dims, ((), ())), preferred_element_type=F32)


def _dot3(a, b, dims):
    ah, al = _split(a, 2)
    bh, bl = _split(b, 2)
    return _mxu(ah, bh, dims) + (_mxu(ah, bl, dims) + _mxu(al, bh, dims))


def _dot(a, b):
    return _dot3(a, b, ((1,), (0,)))


def _dot_nt(a, b):
    return _dot3(a, b, ((1,), (1,)))


def _mask_dot(mask, x):
    m = mask.astype(BF16)
    return sum(_mxu(m, p, ((1,), (0,))) for p in _split(x, 3))


def _dot_mask(x, mask):
    m = mask.astype(BF16)
    return sum(_mxu(p, m, ((1,), (0,))) for p in _split(x, 3))


def _cumsum_maps(g, c):
    row = lax.broadcasted_iota(jnp.int32, (c, c), 0)
    col = lax.broadcasted_iota(jnp.int32, (c, c), 1)
    tril = row >= col
    gb = jnp.broadcast_to(g, (c, c))
    ri = _mask_dot(tril, gb)
    rj = _mask_dot(jnp.ones((c, c), BF16), jnp.where(row <= col, gb, 0.0))
    return ri, rj, tril, row, col


def _gdn_head(q, k, v, g, beta, s, c, levels):
    ri, rj, tril, row, col = _cumsum_maps(g, c)
    decay = jnp.where(tril, jnp.exp(jnp.where(tril, ri - rj, 0.0)), 0.0)
    gc = ri[:, 0:1]
    gc_last = ri[c - 1:c, 0:1]
    kb = k * beta
    m = jnp.where(row > col, _dot_nt(kb, k) * decay, 0.0)
    eye = (row == col).astype(F32)
    blk = (row // 8) == (col // 8)
    p = jnp.where(blk, m, 0.0)
    x = eye - p
    for _ in range(levels[0]):
        p = _dot(p, p)
        x = x + _dot(x, p)
    b = 8
    for _ in range(levels[1]):
        low = jnp.where(((row // (2 * b)) == (col // (2 * b))) & ((row // b) != (col // b)), m, 0.0)
        x = x - _dot(_dot(x, low), x)
        b *= 2
    u_in = _dot(x, v * beta)
    wk_in = _dot(x, kb * jnp.exp(gc))
    qk = _dot_nt(q, k) * decay
    q_dec = q * jnp.exp(gc)
    k_dec = k * jnp.exp(gc_last - gc)
    u = u_in - _dot(wk_in, s)
    o = _dot(q_dec, s) + _dot(qk, u)
    s_new = s * jnp.exp(gc_last) + _dot(k_dec.T, u)
    return o, s_new


def _gdn_kernel(q_ref, k_ref, v_ref, gb_ref, gate_ref, nw_ref, s0_ref, o_ref, sout_ref, s_scr,
                *, c, levels):
    ci = pl.program_id(1)

    @pl.when(ci == 0)
    def _():
        s_scr[...] = s0_ref[...]

    for hd in range(GDN_HEADS):
        sl = slice(hd * GDN_DV, (hd + 1) * GDN_DV)
        o, s_new = _gdn_head(q_ref[:, sl], k_ref[:, sl], v_ref[:, sl], gb_ref[:, hd:hd + 1],
                             gb_ref[:, GDN_HEADS + hd:GDN_HEADS + hd + 1], s_scr[hd], c, levels)
        s_scr[hd] = s_new
        sout_ref[hd] = s_new
        y = o * lax.rsqrt(jnp.mean(o * o, axis=-1, keepdims=True) + EPS) * nw_ref[...]
        gt = gate_ref[:, sl]
        o_ref[:, sl] = (y * (gt * jax.nn.sigmoid(gt))).astype(o_ref.dtype)


def _gdn_scan(qkv, gb, gate, gate_col, nw, s0, n_valid):
    b, l, _ = qkv.shape
    c = CHUNK
    n_eff = min(n_valid, c)
    base = min(n_eff, 8)
    levels = (max(0, math.ceil(math.log2(base)) - 1) if base > 1 else 0,
              max(0, math.ceil(math.log2(n_eff / 8))) if n_eff > 8 else 0)
    col = lambda j: pl.BlockSpec((None, c, MIX_W), lambda bi, ci: (bi, ci, j))
    st = pl.BlockSpec((None, GDN_HEADS, GDN_DK, GDN_DV), lambda bi, ci: (bi, 0, 0, 0))
    return pl.pallas_call(
        functools.partial(_gdn_kernel, c=c, levels=levels),
        grid=(b, l // c),
        in_specs=[col(0), col(1), col(2),
                  pl.BlockSpec((None, c, 8), lambda bi, ci: (bi, ci, 0)),
                  col(gate_col),
                  pl.BlockSpec((1, GDN_DV), lambda bi, ci: (0, 0)),
                  st],
        out_specs=[col(0), st],
        out_shape=[jax.ShapeDtypeStruct((b, l, MIX_W), BF16),
                   jax.ShapeDtypeStruct((b, GDN_HEADS, GDN_DK, GDN_DV), F32)],
        scratch_shapes=[pltpu.VMEM((GDN_HEADS, GDN_DK, GDN_DV), F32)],
        compiler_params=_cparams("parallel", "arbitrary"),
        name="gdn_scan",
    )(qkv, qkv, qkv, gb, gate, nw.reshape(1, GDN_DV), s0)


def _ssd_kernel(x_ref, da_ref, z_ref, dskip_ref, nw_ref, h0_ref, y_ref, hout_ref, h_scr, *, c):
    ci = pl.program_id(1)

    @pl.when(ci == 0)
    def _():
        h_scr[...] = h0_ref[...]

    lane = lax.broadcasted_iota(jnp.int32, (c, 128), 1)
    lane_lo = lane < 64
    hrow = lax.broadcasted_iota(jnp.int32, (128, 128), 0) < 64
    ys = []
    for pi in range(4):
        gmask = (lane_lo if pi < 2 else jnp.logical_not(lane_lo)).astype(F32)
        x = x_ref[:, pi * 128:(pi + 1) * 128]
        bm = x_ref[:, MIX_W:MIX_W + 128] * gmask
        cm = x_ref[:, MIX_W + 128:MIX_W + 256] * gmask
        hst = h_scr[pi]
        cb = _dot_nt(cm, bm)
        y = jnp.zeros((c, 128), F32)
        h_new = jnp.zeros((128, 128), F32)
        for hh in range(2):
            hd = 2 * pi + hh
            dt = da_ref[:, hd:hd + 1]
            da = da_ref[:, 8 + hd:9 + hd]
            ri, rj, tril, _, _ = _cumsum_maps(da, c)
            seg = jnp.where(tril, jnp.exp(jnp.where(tril, ri - rj, 0.0)), 0.0)
            a = ri[:, 0:1]
            a_last = ri[c - 1:c, 0:1]
            hm = lane_lo if hh == 0 else jnp.logical_not(lane_lo)
            xdt = jnp.where(hm, x, 0.0) * dt
            y_h = _dot(cb * seg, xdt) + _dot_nt(cm * jnp.exp(a), hst)
            y = y + jnp.where(hm, y_h, 0.0)
            upd = _dot(xdt.T, bm * jnp.exp(a_last - a))
            hsel = hrow if hh == 0 else jnp.logical_not(hrow)
            h_new = h_new + jnp.where(hsel, hst * jnp.exp(a_last) + upd, 0.0)
        h_scr[pi] = h_new
        hout_ref[pi] = h_new
        ys.append(y)
    y = jnp.concatenate(ys, axis=1) + dskip_ref[...] * x_ref[:, 0:MIX_W]
    z = z_ref[...]
    yz = y * (z * jax.nn.sigmoid(z))
    y_ref[...] = (yz * lax.rsqrt(jnp.mean(yz * yz, axis=-1, keepdims=True) + EPS)
                  * nw_ref[...]).astype(y_ref.dtype)


def _ssd_scan(xbc, da, z, z_col, dskip, nw, h0):
    b, l, w = xbc.shape
    c = CHUNK
    st = pl.BlockSpec((None, 4, 128, 128), lambda bi, ci: (bi, 0, 0, 0))
    vec = pl.BlockSpec((1, MIX_W), lambda bi, ci: (0, 0))
    return pl.pallas_call(
        functools.partial(_ssd_kernel, c=c),
        grid=(b, l // c),
        in_specs=[pl.BlockSpec((None, c, w), lambda bi, ci: (bi, ci, 0)),
                  pl.BlockSpec((None, c, 16), lambda bi, ci: (bi, ci, 0)),
                  pl.BlockSpec((None, c, MIX_W), lambda bi, ci: (bi, ci, z_col)),
                  vec, vec, st],
        out_specs=[pl.BlockSpec((None, c, MIX_W), lambda bi, ci: (bi, ci, 0)), st],
        out_shape=[jax.ShapeDtypeStruct((b, l, MIX_W), BF16),
                   jax.ShapeDtypeStruct((b, 4, 128, 128), F32)],
        scratch_shapes=[pltpu.VMEM((4, 128, 128), F32)],
        compiler_params=_cparams("parallel", "arbitrary"),
        name="ssd_scan",
    )(xbc, da, z, dskip.reshape(1, MIX_W), nw.reshape(1, MIX_W), h0)


def _fox_prompt_kernel(q_ref, k_ref, v_ref, cq_ref, ck_ref, o_ref, kb_scr, vb_scr,
                       m_scr, l_scr, acc_scr, qh_scr, cq_scr, *, tq, tk):
    i = pl.program_id(2)

    @pl.when(i == 0)
    def _():
        kb_scr[...] = k_ref[...].astype(BF16)
        vb_scr[...] = v_ref[...].astype(BF16)

    lane = lax.broadcasted_iota(jnp.int32, (tq, 128), 1)
    q = q_ref[...] * (FOX_SCALE * LOG2E)
    rep = tk // 128
    for hh in range(2):
        hm = (lane < 64) if hh == 0 else (lane >= 64)
        qh_scr[hh] = jnp.where(hm, q, 0.0).astype(BF16)
        cq_scr[hh] = jnp.concatenate([jnp.broadcast_to(cq_ref[:, hh:hh + 1], (tq, 128))] * rep, axis=1)
        m_scr[hh] = jnp.full((tq, 128), -jnp.inf, F32)
        l_scr[hh] = jnp.zeros((tq, 128), F32)
        acc_scr[hh] = jnp.zeros((tq, 128), F32)

    def step(j, masked):
        off = pl.multiple_of(j * tk, tk)
        kk = kb_scr[pl.ds(off, tk), :]
        vv = vb_scr[pl.ds(off, tk), :]
        for hh in range(2):
            s = lax.dot_general(qh_scr[hh], kk, (((1,), (1,)), ((), ())),
                                preferred_element_type=F32)
            s = s + cq_scr[hh] - ck_ref[j, hh:hh + 1, :]
            if masked:
                rowp = lax.broadcasted_iota(jnp.int32, (tq, tk), 0)
                colp = lax.broadcasted_iota(jnp.int32, (tq, tk), 1)
                s = jnp.where(colp <= rowp, s, -jnp.inf)
            m_prev = m_scr[hh]
            m_new = jnp.maximum(m_prev, jnp.max(s, axis=1, keepdims=True))
            alpha = jnp.exp2(m_prev - m_new)
            p = jnp.exp2(s - jnp.concatenate([m_new] * rep, axis=1))
            l_scr[hh] = alpha * l_scr[hh] + jnp.sum(p, axis=1, keepdims=True)
            acc_scr[hh] = alpha * acc_scr[hh] + jnp.dot(p.astype(BF16), vv,
                                                        preferred_element_type=F32)
            m_scr[hh] = m_new

    def body(j, carry):
        step(j, False)
        return carry

    lax.fori_loop(0, i, body, 0)
    step(i, True)
    o_ref[...] = jnp.where(lane < 64, acc_scr[0] / l_scr[0], acc_scr[1] / l_scr[1]).astype(o_ref.dtype)


def _fox_prompt(proj, cum, bsz, seq):
    tq = tk = 512 if seq % 512 == 0 else seq
    nq = seq // tq
    cum = cum * LOG2E
    cq = cum.reshape(bsz, seq, 4, 2).transpose(0, 2, 1, 3)
    ck = cum.reshape(bsz, nq, tk, 4, 2).transpose(0, 3, 1, 4, 2)
    qb = C_BQKV // 128
    kb = (C_BQKV + 512) // 128
    vb = (C_BQKV + 1024) // 128
    return pl.pallas_call(
        functools.partial(_fox_prompt_kernel, tq=tq, tk=tk),
        grid=(bsz, 4, nq),
        in_specs=[pl.BlockSpec((tq, 128), lambda b, p, i: (b * nq + i, qb + p)),
                  pl.BlockSpec((seq, 128), lambda b, p, i: (b, kb + p)),
                  pl.BlockSpec((seq, 128), lambda b, p, i: (b, vb + p)),
                  pl.BlockSpec((None, None, tq, 2), lambda b, p, i: (b, p, i, 0)),
                  pl.BlockSpec((None, None, nq, 2, tk), lambda b, p, i: (b, p, 0, 0, 0))],
        out_specs=pl.BlockSpec((tq, 128), lambda b, p, i: (b * nq + i, p)),
        out_shape=jax.ShapeDtypeStruct((bsz * seq, 512), BF16),
        scratch_shapes=[pltpu.VMEM((seq, 128), BF16), pltpu.VMEM((seq, 128), BF16),
                        pltpu.VMEM((2, tq, 128), F32), pltpu.VMEM((2, tq, 128), F32),
                        pltpu.VMEM((2, tq, 128), F32), pltpu.VMEM((2, tq, 128), BF16),
                        pltpu.VMEM((2, tq, tk), F32)],
        compiler_params=_cparams("parallel", "parallel", "arbitrary"),
        name="fox_prompt",
    )(proj, proj, proj, cq, ck)


def _fox_sample_kernel(pt_ref, q_ref, kn_ref, vn_ref, ln_ref, *rest, n_new, pps):
    page_refs = [rest[3 * i:3 * i + 3] for i in range(pps)]
    o_ref, m_scr, l_scr, acc_scr, cs_scr, suf_scr = rest[3 * pps:]
    s_id = pl.program_id(1)
    n_steps = pl.num_programs(1)
    r = 8 * n_new
    rowi = lax.broadcasted_iota(jnp.int32, (r, MIX_W), 0)
    lanei = lax.broadcasted_iota(jnp.int32, (r, MIX_W), 1)
    hmask = (lanei // FOX_DH) == (rowi % 8)
    q = q_ref[...] * FOX_SCALE
    qrep = jnp.concatenate([jnp.broadcast_to(q[t:t + 1, :], (8, MIX_W)) for t in range(n_new)], axis=0)
    qbd = jnp.where(hmask, qrep, 0.0).astype(BF16)
    krow = lax.broadcasted_iota(jnp.int32, (128, 128), 0)
    kcol = lax.broadcasted_iota(jnp.int32, (128, 128), 1)
    rr = lax.broadcasted_iota(jnp.int32, (r, 128), 0)
    kk = lax.broadcasted_iota(jnp.int32, (r, 128), 1)

    def attend(parts):
        ss = []
        for kx, _, bias, valid in parts:
            s = jnp.dot(qbd, kx.astype(BF16), preferred_element_type=F32) + bias
            ss.append(s if valid is None else jnp.where(valid, s, -jnp.inf))
        m_new = m_prev = m_scr[...]
        for s in ss:
            m_new = jnp.maximum(m_new, jnp.max(s, axis=1, keepdims=True))
        alpha = jnp.exp(m_prev - m_new)
        l_new = alpha * l_scr[...]
        acc = jnp.concatenate([alpha] * (MIX_W // 128), axis=1) * acc_scr[...]
        for (_, vx, _, _), s in zip(parts, ss):
            p = jnp.exp(s - m_new)
            l_new = l_new + jnp.sum(p, axis=1, keepdims=True)
            acc = acc + lax.dot_general(p.astype(BF16), vx.astype(BF16), (((1,), (1,)), ((), ())),
                                        preferred_element_type=F32)
        l_scr[...] = l_new
        acc_scr[...] = acc
        m_scr[...] = m_new

    @pl.when(s_id == 0)
    def _():
        m_scr[...] = jnp.full((r, 128), -jnp.inf, F32)
        l_scr[...] = jnp.zeros((r, 128), F32)
        acc_scr[...] = jnp.zeros((r, MIX_W), F32)
        suf_scr[...] = jnp.zeros((r, 128), F32)
        lf = jnp.concatenate([ln_ref[...]] * n_new, axis=0)
        cum = _dot_mask(lf, krow <= kcol)
        tq = rr // 8
        cs = jnp.sum(jnp.where(kk == tq, cum, 0.0), axis=1, keepdims=True)
        cs_scr[...] = jnp.broadcast_to(cs, (r, 128))
        attend([(kn_ref[...], vn_ref[...], cs - cum, kk <= tq)])

    @pl.when(s_id > 0)
    def _():
        suf = suf_scr[...]
        cs = cs_scr[...]
        parts = []
        for kp_ref, vp_ref, lp_ref in page_refs:
            lf = jnp.concatenate([lp_ref[...]] * n_new, axis=0)
            after = _dot_mask(lf, krow > kcol)
            parts.append((kp_ref[...].reshape(MIX_W, PAGE_SIZE), vp_ref[...].reshape(MIX_W, PAGE_SIZE),
                          cs + suf + after, None))
            suf = suf + jnp.sum(lf, axis=1, keepdims=True)
        attend(parts)
        suf_scr[...] = suf

    @pl.when(s_id == n_steps - 1)
    def _():
        o = jnp.where(hmask, acc_scr[...] / jnp.concatenate([l_scr[...]] * (MIX_W // 128), axis=1), 0.0)
        o_ref[...] = jnp.sum(o.reshape(n_new, 8, MIX_W), axis=1)


def _fox_sample(q, k_new, v_new, logf_new, cache_k, cache_v, cache_lf, page_table, layer):
    bd, n_new, _ = q.shape
    n_pages = page_table.shape[1]
    r = 8 * n_new
    key_last = lambda a: _pad_seq(a, 1, PAGE_SIZE).transpose(0, 2, 1)
    pps = 2 if n_pages % 2 == 0 else 1

    def page(i):
        return lambda b, s, pt: pt[b, n_pages - 1 - ((jnp.maximum(s, 1) - 1) * pps + i)]

    per_b = lambda shape: pl.BlockSpec((None,) + shape, lambda b, s, pt: (b, 0, 0))
    page_specs, page_args = [], []
    for i in range(pps):
        pg = page(i)
        kv = pl.BlockSpec((None, None, FOX_HEADS, FOX_DH, PAGE_SIZE),
                          lambda b, s, pt, pg=pg: (layer, pg(b, s, pt), 0, 0, 0))
        lf = pl.BlockSpec((None, None, FOX_HEADS, PAGE_SIZE),
                          lambda b, s, pt, pg=pg: (layer, pg(b, s, pt), 0, 0))
        page_specs += [kv, kv, lf]
        page_args += [cache_k, cache_v, cache_lf]
    grid_spec = pltpu.PrefetchScalarGridSpec(
        num_scalar_prefetch=1,
        grid=(bd, n_pages // pps + 1),
        in_specs=[per_b((n_new, MIX_W)), per_b((MIX_W, PAGE_SIZE)), per_b((MIX_W, PAGE_SIZE)),
                  per_b((FOX_HEADS, PAGE_SIZE))] + page_specs,
        out_specs=per_b((n_new, MIX_W)),
        scratch_shapes=[pltpu.VMEM((r, 128), F32), pltpu.VMEM((r, 128), F32),
                        pltpu.VMEM((r, MIX_W), F32), pltpu.VMEM((r, 128), F32),
                        pltpu.VMEM((r, 128), F32)])
    o = pl.pallas_call(
        functools.partial(_fox_sample_kernel, n_new=n_new, pps=pps),
        grid_spec=grid_spec,
        out_shape=jax.ShapeDtypeStruct((bd, n_new, MIX_W), F32),
        compiler_params=_cparams("parallel", "arbitrary"),
        name="fox_sample",
    )(page_table, q, key_last(k_new), key_last(v_new), key_last(logf_new), *page_args)
    return o.reshape(bd * n_new, MIX_W)


def _merge_kernel(oa_ref, ob_ref, oc_ref, g0_ref, g1_ref, g2_ref, wb_ref, wo_ref, x_ref,
                  gate_ref, nw_ref, sc_ref, sh_ref, xo_ref, h_ref):
    acc = None
    for br, (o_r, g_r) in enumerate(((oa_ref, g0_ref), (ob_ref, g1_ref), (oc_ref, g2_ref))):
        y = jnp.dot(o_r[...].astype(BF16), wb_ref[br], preferred_element_type=F32)
        y = jax.nn.sigmoid(g_r[...]) * y
        acc = y if acc is None else acc + y
    out = jnp.dot(acc.astype(BF16), wo_ref[...], preferred_element_type=F32)
    xn = x_ref[...] + gate_ref[...] * out
    xo_ref[...] = xn
    h_ref[...] = _rms_mod(xn, nw_ref[...], sc_ref[...], sh_ref[...])


def _merge(o_a, o_b, o_c, proj, wb, wo, x, g1, nw, sc2, sh2, seq_len):
    t, d = x.shape
    tm = min(256, t)
    g_a, g_s = _mod_operand(g1, seq_len, tm)
    sc_a, sc_s = _mod_operand(sc2, seq_len, tm)
    sh_a, sh_s = _mod_operand(sh2, seq_len, tm)
    br = pl.BlockSpec((tm, MIX_W), lambda i: (i, 0))
    row = pl.BlockSpec((tm, d), lambda i: (i, 0))
    return pl.pallas_call(
        _merge_kernel,
        grid=(t // tm,),
        in_specs=[br, br, br,
                  pl.BlockSpec((tm, d), lambda i: (i, 0)),
                  pl.BlockSpec((tm, d), lambda i: (i, 1)),
                  pl.BlockSpec((tm, d), lambda i: (i, 2)),
                  pl.BlockSpec((3, MIX_W, d), lambda i: (0, 0, 0)),
                  pl.BlockSpec((d, d), lambda i: (0, 0)),
                  row, g_s,
                  pl.BlockSpec((1, d), lambda i: (0, 0)),
                  sc_s, sh_s],
        out_specs=[row, row],
        out_shape=[jax.ShapeDtypeStruct((t, d), F32), jax.ShapeDtypeStruct((t, d), F32)],
        compiler_params=_cparams("parallel"),
        name="merge",
    )(o_a, o_b, o_c, proj, proj, proj, wb, wo, x, g_a, nw.reshape(1, d), sc_a, sh_a)


def _top16_rows(s):
    rows, tb = s.shape
    rid = lax.broadcasted_iota(jnp.int32, (rows, tb), 0).astype(F32)
    out_row = lax.broadcasted_iota(jnp.int32, (PEER_TOPK, tb), 0)
    vals = jnp.zeros((PEER_TOPK, tb), F32)
    idxs = jnp.zeros((PEER_TOPK, tb), F32)
    for it in range(PEER_TOPK):
        mx = jnp.max(s, axis=0, keepdims=True)
        ix = jnp.min(jnp.where(s == mx, rid, float(rows)), axis=0, keepdims=True)
        vals = jnp.where(out_row == it, mx, vals)
        idxs = jnp.where(out_row == it, ix, idxs)
        s = jnp.where(rid == ix, -jnp.inf, s)
    return vals, idxs


_CAND_GROUPS = (("i", 0, 0, 16), ("j", 0, 1, 16), ("i", 1, 1, 8), ("j", 1, 2, 8),
                ("i", 2, 2, 5), ("i", 3, 2, 4), ("i", 4, 2, 3))


def _candidates(a, b, ia, ib):
    tb = a.shape[1]
    vals, poss, cids = [], [], []
    for kind, fixed, lo, hi in _CAND_GROUPS:
        n = 16 if hi > 8 else 8
        r = lax.broadcasted_iota(jnp.int32, (n, tb), 0)
        ok = (r >= lo) & (r < hi)
        if kind == "i":
            v = a[fixed:fixed + 1] + b[0:n]
            c = ia[fixed:fixed + 1] * float(PEER_NKEYS) + ib[0:n]
            p = fixed * PEER_TOPK + r
        else:
            v = a[0:n] + b[fixed:fixed + 1]
            c = ia[0:n] * float(PEER_NKEYS) + ib[fixed:fixed + 1]
            p = r * PEER_TOPK + fixed
        vals.append(jnp.where(ok, v, -jnp.inf))
        poss.append(jnp.where(ok, p, 4096).astype(F32))
        cids.append(c)
    return (jnp.concatenate(vals, axis=0), jnp.concatenate(poss, axis=0),
            jnp.concatenate(cids, axis=0))


def _route_kernel(h_ref, wq_ref, keys_ref, eid_ref, gate_ref, *, tb):
    qt = lax.dot_general(wq_ref[...], h_ref[...].astype(BF16), (((1,), (1,)), ((), ())),
                         preferred_element_type=F32)
    out_row = lax.broadcasted_iota(jnp.int32, (PEER_TOPK, tb), 0)
    tops, eids = [], []
    for hd in range(PEER_HEADS):
        sv, si = [], []
        for half in range(2):
            c0 = (hd * 2 + half) * 128
            sc = jnp.dot(keys_ref[half], qt[c0:c0 + 128, :].astype(BF16),
                         preferred_element_type=F32)
            v_, i_ = _top16_rows(sc)
            sv.append(v_)
            si.append(i_)
        cand, cpos, cid = _candidates(sv[0], sv[1], si[0], si[1])
        top = jnp.zeros((PEER_TOPK, tb), F32)
        eid = jnp.zeros((PEER_TOPK, tb), F32)
        for it in range(PEER_TOPK):
            mx = jnp.max(cand, axis=0, keepdims=True)
            px = jnp.min(jnp.where(cand == mx, cpos, 8192.0), axis=0, keepdims=True)
            hit = cpos == px
            ev = jnp.sum(jnp.where(hit, cid, 0.0), axis=0, keepdims=True)
            top = jnp.where(out_row == it, mx, top)
            eid = jnp.where(out_row == it, ev, eid)
            cand = jnp.where(hit, -jnp.inf, cand)
        e = jnp.exp(top - top[0:1])
        tops.append(e / jnp.sum(e, axis=0, keepdims=True))
        eids.append(eid)
    gate_ref[...] = jnp.concatenate(tops, axis=0)
    eid_ref[...] = jnp.concatenate(eids, axis=0).T.astype(jnp.int32)


def _route(h, wq_t, keys):
    t, d = h.shape
    tb = 128
    return pl.pallas_call(
        functools.partial(_route_kernel, tb=tb),
        grid=(t // tb,),
        in_specs=[pl.BlockSpec((tb, d), lambda i: (i, 0)),
                  pl.BlockSpec((PEER_HEADS * PEER_DQ, d), lambda i: (0, 0)),
                  pl.BlockSpec((2, 128, 128), lambda i: (0, 0, 0))],
        out_specs=[pl.BlockSpec((tb, 128), lambda i: (i, 0)),
                   pl.BlockSpec((None, 128, tb), lambda i: (i, 0, 0))],
        out_shape=[jax.ShapeDtypeStruct((t, 128), jnp.int32),
                   jax.ShapeDtypeStruct((t // tb, 128, tb), F32)],
        compiler_params=_cparams("parallel"),
        name="peer_route",
    )(h, wq_t, keys)


GATHER_WINDOW = 64


def _sc_gather(table, idx):
    n = idx.shape[0]
    width = table.shape[1]
    mesh = plsc.VectorSubcoreMesh(core_axis_name="core", subcore_axis_name="subcore")
    n_workers = mesh.num_cores * mesh.num_subcores
    win = GATHER_WINDOW
    per_w = n // n_workers
    n_chunks = per_w // win
    assert per_w * n_workers == n and n_chunks * win == per_w and n_chunks % 2 == 0

    @pl.kernel(out_type=jax.ShapeDtypeStruct((n, width), table.dtype), mesh=mesh,
               scratch_types=[pltpu.VMEM((win,), jnp.int32), pltpu.VMEM((win,), jnp.int32),
                              pltpu.VMEM((win, width), table.dtype),
                              pltpu.VMEM((win, width), table.dtype),
                              pltpu.SemaphoreType.DMA, pltpu.SemaphoreType.DMA,
                              pltpu.SemaphoreType.DMA, pltpu.SemaphoreType.DMA],
               name="peer_gather")
    def gather_kernel(tab_hbm, idx_hbm, out_hbm, i0, i1, r0, r1, g0, g1, w0, w1):
        wid = lax.axis_index("subcore") * mesh.num_cores + lax.axis_index("core")
        base = wid * per_w
        bufs = ((i0, r0, g0, w0), (i1, r1, g1, w1))

        def write_copy(rv, ws, off):
            return pltpu.make_async_copy(rv, out_hbm.at[pl.ds(off, win)], ws)

        @pl.loop(0, n_chunks, step=2)
        def _(c):
            for b, (iv, rv, gs, ws) in enumerate(bufs):
                off = base + (c + b) * win

                @pl.when(c > 0)
                def _():
                    write_copy(rv, ws, off).wait()

                pltpu.sync_copy(idx_hbm.at[pl.ds(off, win)], iv)
                pltpu.make_async_copy(tab_hbm.at[iv], rv, gs).start()
            for b, (iv, rv, gs, ws) in enumerate(bufs):
                off = base + (c + b) * win
                pltpu.make_async_copy(tab_hbm.at[iv], rv, gs).wait()
                write_copy(rv, ws, off).start()

        for iv, rv, gs, ws in bufs:
            write_copy(rv, ws, base).wait()

    return gather_kernel(table, idx)


def _pack_table(tab):
    b = lax.bitcast_convert_type(tab.astype(BF16), jnp.uint16).astype(jnp.uint32)
    return lax.bitcast_convert_type(b[:, :512] | (b[:, 512:] << 16), jnp.int32)


def _unpack(w):
    lo = pltpu.bitcast(w << 16, F32)
    hi = pltpu.bitcast(w & jnp.int32(-65536), F32)
    return lo, hi


def _peer_kernel(gu_ref, gv_ref, h_ref, gt_ref, x_ref, g2_ref, o_ref, *, tb):
    base = (pl.program_id(0) * tb) % 128
    gt = gt_ref[...]
    tok = lax.broadcasted_iota(jnp.int32, (128, 128), 1)
    for t in range(tb):
        h_lo = h_ref[t:t + 1, 0:512]
        h_hi = h_ref[t:t + 1, 512:1024]
        u_lo, u_hi = _unpack(gu_ref[t * 128:(t + 1) * 128, :])
        pu = u_lo * h_lo + u_hi * h_hi
        p128 = pu[:, 0:128] + pu[:, 128:256] + pu[:, 256:384] + pu[:, 384:512]
        pre = jnp.sum(p128, axis=1, keepdims=True)
        act = 0.5 * pre * (1.0 + lax.erf(pre * (2.0 ** -0.5)))
        gcol = jnp.sum(jnp.where(tok == base + t, gt, 0.0), axis=1, keepdims=True)
        w = gcol * act
        v_lo, v_hi = _unpack(gv_ref[t * 128:(t + 1) * 128, :])
        y_lo = jnp.sum(v_lo * w, axis=0, keepdims=True)
        y_hi = jnp.sum(v_hi * w, axis=0, keepdims=True)
        o_ref[t:t + 1, 0:512] = x_ref[t:t + 1, 0:512] + g2_ref[t:t + 1, 0:512] * y_lo
        o_ref[t:t + 1, 512:1024] = x_ref[t:t + 1, 512:1024] + g2_ref[t:t + 1, 512:1024] * y_hi


def _peer_mix(gu, gv, h, gate_t, x, g2, seq_len):
    t, d = x.shape
    tb = 16
    if seq_len % tb == 0:
        per = seq_len // tb
        g_a = jnp.broadcast_to(g2.reshape(g2.shape[0], 1, d), (g2.shape[0], tb, d))
        g_s = pl.BlockSpec((None, tb, d), lambda i: (i // per, 0, 0))
    else:
        g_a = jnp.repeat(g2, seq_len, axis=0)
        g_s = pl.BlockSpec((tb, d), lambda i: (i, 0))
    row = pl.BlockSpec((tb, d), lambda i: (i, 0))
    rows = pl.BlockSpec((tb * 128, 512), lambda i: (i, 0))
    per_grp = 128 // tb
    return pl.pallas_call(
        functools.partial(_peer_kernel, tb=tb),
        grid=(t // tb,),
        in_specs=[rows, rows, row,
                  pl.BlockSpec((None, 128, 128), lambda i: (i // per_grp, 0, 0)),
                  row, g_s],
        out_specs=row,
        out_shape=jax.ShapeDtypeStruct((t, d), F32),
        compiler_params=_cparams("parallel"),
        name="peer_mix",
    )(gu, gv, h, gate_t, x, g_a)


def _final_norm_kernel(x_ref, w_ref, o_ref):
    x = x_ref[...]
    o_ref[...] = x * lax.rsqrt(jnp.mean(x * x, axis=-1, keepdims=True) + EPS) * w_ref[...]


def _final_norm(x, w):
    t, d = x.shape
    tm = min(512, t)
    return pl.pallas_call(
        _final_norm_kernel,
        grid=(t // tm,),
        in_specs=[pl.BlockSpec((tm, d), lambda i: (i, 0)), pl.BlockSpec((1, d), lambda i: (0, 0))],
        out_specs=pl.BlockSpec((tm, d), lambda i: (i, 0)),
        out_shape=jax.ShapeDtypeStruct((t, d), F32),
        compiler_params=_cparams("parallel"),
        name="final_norm",
    )(x, w.reshape(1, d))


def _causal_conv(x, buf, w):
    l = x.shape[1]
    xp = jnp.concatenate([buf.astype(x.dtype), x], axis=1)
    y = sum(xp[:, i:i + l] * w[i] for i in range(CONV_W))
    return y, xp[:, l:]


def _l2norm(x):
    return x * lax.rsqrt(jnp.sum(x * x, axis=-1, keepdims=True) + EPS)


def _pad_seq(a, axis, lp):
    pad = lp - a.shape[axis]
    if pad == 0:
        return a
    cfg = [(0, 0)] * a.ndim
    cfg[axis] = (0, pad)
    return jnp.pad(a, cfg)


def _layer(x, mod, lw, states, bsz, seq, fox):
    sh1, sc1, g1, sh2, sc2, g2 = mod
    gdn_s, gdn_b, ssm_s, ssm_b = states
    t = bsz * seq
    lp = -(-seq // CHUNK) * CHUNK
    proj = _proj_in(x, lw["norm_mix"], sc1, sh1, lw["w_in"], seq)
    p3 = proj.reshape(bsz, seq, N_PROJ)
    small = p3[:, :, C_SMALL:C_SMALL + 24]
    a_beta, a_alpha, b_f, c_dt = small[..., 0:4], small[..., 4:8], small[..., 8:16], small[..., 16:24]

    a_conv, gdn_b_new = _causal_conv(p3[:, :, C_AQKV:C_AQKV + GDN_CONV_CH], gdn_b, lw["gdn_conv_w"])
    a_act = jax.nn.silu(a_conv)
    q = _l2norm(a_act[..., 0:512].reshape(bsz, seq, GDN_HEADS, GDN_DK)) * GDN_DK ** -0.5
    k = _l2norm(a_act[..., 512:1024].reshape(bsz, seq, GDN_HEADS, GDN_DK))
    qkv = jnp.concatenate([q.reshape(bsz, seq, MIX_W), k.reshape(bsz, seq, MIX_W),
                           a_act[..., 1024:1536]], axis=-1)
    beta = jax.nn.sigmoid(a_beta)
    g = -jnp.exp(lw["gdn_a_log"]) * jax.nn.softplus(a_alpha + lw["gdn_dt_bias"])
    gb = jnp.concatenate([g, beta], axis=-1)
    if seq % CHUNK == 0:
        gate_arr, gate_col = p3, C_AGATE // MIX_W
    else:
        gate_arr, gate_col = _pad_seq(p3[:, :, C_AGATE:C_AGATE + MIX_W], 1, lp), 0
    o_a, gdn_s_new = _gdn_scan(_pad_seq(qkv, 1, lp), _pad_seq(gb, 1, lp), gate_arr, gate_col,
                               lw["gdn_norm_w"], gdn_s, seq)
    o_a = o_a[:, :seq].reshape(t, MIX_W)

    logf = jax.nn.log_sigmoid(b_f + lw["fox_f_bias"])
    kf = p3[:, :, C_BQKV + 512:C_BQKV + 1024].reshape(bsz, seq, FOX_HEADS, FOX_DH)
    vf = p3[:, :, C_BQKV + 1024:C_BQKV + 1536].reshape(bsz, seq, FOX_HEADS, FOX_DH)
    o_b = fox(proj, p3, logf)

    xbc, ssm_b_new = _causal_conv(p3[:, :, C_CXBC:C_CXBC + SSM_CONV_CH], ssm_b, lw["ssm_conv_w"])
    xbc = jax.nn.silu(xbc + lw["ssm_conv_b"])
    dt = jax.nn.softplus(c_dt + lw["ssm_dt_bias"])
    da = jnp.concatenate([dt, dt * (-jnp.exp(lw["ssm_a_log"]))], axis=-1)
    if seq % CHUNK == 0:
        z_arr, z_col = p3, C_CZ // MIX_W
    else:
        z_arr, z_col = _pad_seq(p3[:, :, C_CZ:C_CZ + MIX_W], 1, lp), 0
    o_c, ssm_s_new = _ssd_scan(_pad_seq(xbc, 1, lp), _pad_seq(da, 1, lp), z_arr, z_col,
                               jnp.repeat(lw["ssm_d"], SSM_P), lw["ssm_norm_w"], ssm_s)
    o_c = o_c[:, :seq].reshape(t, MIX_W)

    x, h2 = _merge(o_a, o_b, o_c, proj, lw["w_branch"], lw["w_out"], x, g1, lw["norm_ffn"],
                   sc2, sh2, seq)
    eid, gate_t = _route(h2, lw["peer_wq_t"], lw["peer_keys"])
    flat = eid.reshape(-1)
    gu = _sc_gather(lw["peer_u"], flat)
    gv = _sc_gather(lw["peer_v"], flat)
    x = _peer_mix(gu, gv, h2, gate_t, x, g2, seq)
    return x, (kf, vf, logf, gdn_s_new, gdn_b_new, ssm_s_new, ssm_b_new)


def _ssm_state_pack(s):
    b = s.shape[0]
    sp = s.reshape(b, 4, 128, SSM_N)
    z = jnp.zeros_like(sp)
    lo = jnp.concatenate([sp, z], axis=-1)
    hi = jnp.concatenate([z, sp], axis=-1)
    grp = (jnp.arange(4) // 2).reshape(1, 4, 1, 1)
    return jnp.where(grp == 0, lo, hi)


def _ssm_state_unpack(sp):
    b = sp.shape[0]
    out = jnp.where((jnp.arange(4) // 2).reshape(1, 4, 1, 1) == 0, sp[..., :SSM_N], sp[..., SSM_N:])
    return out.reshape(b, SSM_HEADS, SSM_P, SSM_N)


def kernel(x_prompt, x_sample, cache_k, cache_v, cache_logf, page_table, state_gdn, state_gdn_conv,
           state_ssm, state_ssm_conv, c_prompt, c_sample, w_ada, b_ada, norm_mix, norm_ffn, w_in,
           gdn_conv_w, gdn_a_log, gdn_dt_bias, gdn_norm_w, fox_f_bias, ssm_conv_w, ssm_conv_b,
           ssm_a_log, ssm_dt_bias, ssm_d, ssm_norm_w, w_branch, w_out, peer_wq, peer_keys, peer_u,
           peer_v, final_norm):
    n_pr, seq, d = x_prompt.shape
    n_dec, dec_seq, _ = x_sample.shape
    n_pool = cache_k.shape[1]
    xp = x_prompt.reshape(n_pr * seq, d)
    xs = x_sample.reshape(n_dec * dec_seq, d)
    c_all = jnp.concatenate([c_prompt, c_sample], axis=0)
    n_c = c_all.shape[0]
    c_all = jnp.pad(c_all, ((0, -n_c % 8), (0, 0)))
    cache_kt = cache_k.transpose(0, 1, 3, 4, 2)
    cache_vt = cache_v.transpose(0, 1, 3, 4, 2)
    cache_lf = cache_logf.transpose(0, 1, 3, 2)
    new_p, new_s = [], []
    for l in range(DEPTH):
        lw = {"norm_mix": norm_mix[l], "norm_ffn": norm_ffn[l], "w_in": _pack_w_in(w_in[l]),
              "gdn_conv_w": gdn_conv_w[l], "gdn_a_log": gdn_a_log[l], "gdn_dt_bias": gdn_dt_bias[l],
              "gdn_norm_w": gdn_norm_w[l], "fox_f_bias": fox_f_bias[l], "ssm_conv_w": ssm_conv_w[l],
              "ssm_conv_b": ssm_conv_b[l], "ssm_a_log": ssm_a_log[l], "ssm_dt_bias": ssm_dt_bias[l],
              "ssm_d": ssm_d[l], "ssm_norm_w": ssm_norm_w[l], "w_branch": w_branch[l].astype(BF16),
              "w_out": w_out[l].astype(BF16), "peer_wq_t": peer_wq[l].T.astype(BF16),
              "peer_keys": peer_keys[l].astype(BF16),
              "peer_u": _pack_table(peer_u[l]), "peer_v": _pack_table(peer_v[l])}
        mod = _adaln(c_all, w_ada[l].astype(BF16), b_ada[l])
        mod_p = jnp.split(mod[:n_pr], 6, axis=-1)
        mod_s = jnp.split(mod[n_pr:n_c], 6, axis=-1)

        def fox_p(proj, p3, logf):
            return _fox_prompt(proj, jnp.cumsum(logf, axis=1), n_pr, seq)

        zero_states = (jnp.zeros((n_pr, GDN_HEADS, GDN_DK, GDN_DV), F32),
                       jnp.zeros((n_pr, CONV_W - 1, GDN_CONV_CH), F32),
                       jnp.zeros((n_pr, 4, 128, 128), F32),
                       jnp.zeros((n_pr, CONV_W - 1, SSM_CONV_CH), F32))
        xp, st = _layer(xp, mod_p, lw, zero_states, n_pr, seq, fox_p)
        new_p.append(st[:5] + (_ssm_state_unpack(st[5]), st[6]))

        def fox_s(proj, p3, logf, l=l):
            qkv = [p3[:, :, C_BQKV + i * MIX_W:C_BQKV + (i + 1) * MIX_W] for i in range(3)]
            return _fox_sample(*qkv, logf, cache_kt, cache_vt, cache_lf, page_table, l).astype(BF16)

        s_states = (state_gdn[l], state_gdn_conv[l], _ssm_state_pack(state_ssm[l]), state_ssm_conv[l])
        xs, st = _layer(xs, mod_s, lw, s_states, n_dec, dec_seq, fox_s)
        new_s.append(st[:5] + (_ssm_state_unpack(st[5]), st[6]))
    y_prompt = _final_norm(xp, final_norm).reshape(n_pr, seq, d)
    y_sample = _final_norm(xs, final_norm).reshape(n_dec, dec_seq, d)
    k_p, v_p, f_p, g_p, gc_p, s_p, sc_p = [jnp.stack(z) for z in zip(*new_p)]
    k_s, v_s, f_s, g_s, gc_s, s_s, sc_s = [jnp.stack(z) for z in zip(*new_s)]
    return (y_prompt, y_sample, k_p, v_p, f_p, g_p, gc_p, s_p, sc_p,
            k_s, v_s, f_s, g_s, gc_s, s_s, sc_s)
```

```python
import functools
import math

import jax
import jax.numpy as jnp
from jax import lax
from jax.experimental import pallas as pl
from jax.experimental.pallas import tpu as pltpu
from jax.experimental.pallas import tpu_sc as plsc

F32 = jnp.float32
BF16 = jnp.bfloat16

D_MODEL = 1024
DEPTH = 4
PAGE_SIZE = 128
MIX_W = D_MODEL // 2
CONV_W = 4
GDN_DK = 128
GDN_DV = 128
GDN_HEADS = MIX_W // GDN_DV
FOX_DH = 64
FOX_HEADS = MIX_W // FOX_DH
FOX_SCALE = FOX_DH ** -0.5
SSM_P = 64
SSM_HEADS = MIX_W // SSM_P
SSM_GROUPS = 2
SSM_N = 64
PEER_HEADS = 8
PEER_NKEYS = 128
PEER_DQ = 256
PEER_TOPK = 16
EPS = 1e-6

GDN_CONV_CH = 2 * GDN_HEADS * GDN_DK + GDN_HEADS * GDN_DV
SSM_CONV_CH = MIX_W + 2 * SSM_GROUPS * SSM_N
IN_SPLITS = (GDN_CONV_CH, MIX_W, GDN_HEADS, GDN_HEADS, 3 * MIX_W, FOX_HEADS,
             MIX_W, SSM_CONV_CH, SSM_HEADS, 3 * D_MODEL)

C_GATES = 0
C_AQKV = 3072
C_AGATE = 4608
C_BQKV = 5120
C_CZ = 6656
C_CXBC = 7168
C_SMALL = 7936
N_PROJ = 8192

LOG2E = 1.4426950408889634
CHUNK = 128
VMEM_LIMIT = 48 * 1024 * 1024


def _cparams(*sem):
    return pltpu.CompilerParams(dimension_semantics=sem, vmem_limit_bytes=VMEM_LIMIT)


def _pack_w_in(w_in):
    offs = [0]
    for s in IN_SPLITS:
        offs.append(offs[-1] + s)
    a_qkv, a_gate, a_beta, a_alpha, b_qkv, b_f, c_z, c_xbc, c_dt, gates = [
        w_in[:, offs[i]:offs[i + 1]] for i in range(len(IN_SPLITS))]
    small = jnp.concatenate([a_beta, a_alpha, b_f, c_dt], axis=1)
    small = jnp.pad(small, ((0, 0), (0, 128 - small.shape[1])))
    pad = jnp.zeros((w_in.shape[0], N_PROJ - C_SMALL - 128), w_in.dtype)
    out = jnp.concatenate([gates, a_qkv, a_gate, b_qkv, c_z, c_xbc, small, pad], axis=1)
    return out.astype(BF16)


def _ada_kernel(c_ref, w_ref, b_ref, o_ref):
    c = c_ref[...]
    a = (c * jax.nn.sigmoid(c)).astype(BF16)
    o_ref[...] = jnp.dot(a, w_ref[...], preferred_element_type=F32) + b_ref[...]


def _adaln(c, w, b):
    m, k = c.shape
    n = w.shape[1]
    tn = 512
    return pl.pallas_call(
        _ada_kernel,
        grid=(n // tn,),
        in_specs=[pl.BlockSpec((m, k), lambda j: (0, 0)),
                  pl.BlockSpec((k, tn), lambda j: (0, j)),
                  pl.BlockSpec((1, tn), lambda j: (0, j))],
        out_specs=pl.BlockSpec((m, tn), lambda j: (0, j)),
        out_shape=jax.ShapeDtypeStruct((m, n), F32),
        compiler_params=_cparams("parallel"),
        name="adaln",
    )(c, w, b.reshape(1, n))


def _mod_operand(m, seq_len, tm):
    bsz, d = m.shape
    if seq_len % tm == 0:
        per = seq_len // tm
        return m.reshape(bsz, 1, d), pl.BlockSpec((None, 1, d), lambda i, *_: (i // per, 0, 0))
    arr = jnp.repeat(m, seq_len, axis=0)
    return arr, pl.BlockSpec((tm, d), lambda i, *_: (i, 0))


def _rms_mod(x, nw, sc, sh):
    y = x * lax.rsqrt(jnp.mean(x * x, axis=-1, keepdims=True) + EPS)
    return (y * nw) * (1.0 + sc) + sh


def _proj_kernel(x_ref, nw_ref, sc_ref, sh_ref, w_ref, o_ref, h_scr):
    @pl.when(pl.program_id(1) == 0)
    def _():
        h_scr[...] = _rms_mod(x_ref[...], nw_ref[...], sc_ref[...], sh_ref[...]).astype(BF16)

    o_ref[...] = jnp.dot(h_scr[...], w_ref[...], preferred_element_type=F32)


def _proj_in(x, nw, sc, sh, w, seq_len):
    t, d = x.shape
    n = w.shape[1]
    tm = min(512, t)
    tn = 1024
    sc_a, sc_s = _mod_operand(sc, seq_len, tm)
    sh_a, sh_s = _mod_operand(sh, seq_len, tm)
    return pl.pallas_call(
        _proj_kernel,
        grid=(t // tm, n // tn),
        in_specs=[pl.BlockSpec((tm, d), lambda i, j: (i, 0)),
                  pl.BlockSpec((1, d), lambda i, j: (0, 0)),
                  sc_s, sh_s,
                  pl.BlockSpec((d, tn), lambda i, j: (0, j))],
        out_specs=pl.BlockSpec((tm, tn), lambda i, j: (i, j)),
        out_shape=jax.ShapeDtypeStruct((t, n), F32),
        scratch_shapes=[pltpu.VMEM((tm, d), BF16)],
        compiler_params=_cparams("parallel", "arbitrary"),
        name="proj_in",
    )(x, nw.reshape(1, d), sc_a, sh_a, w)


def _split(x, parts):
    out = []
    for _ in range(parts - 1):
        hi = x.astype(BF16)
        out.append(hi)
        x = x - hi.astype(F32)
    out.append(x.astype(BF16))
    return out


def _mxu(a, b, dims):
    return lax.dot_general(a, b, (dims, ((), ())), preferred_element_type=F32)


def _dot3(a, b, dims):
    ah, al = _split(a, 2)
    bh, bl = _split(b, 2)
    return _mxu(ah, bh, dims) + (_mxu(ah, bl, dims) + _mxu(al, bh, dims))


def _dot(a, b):
    return _dot3(a, b, ((1,), (0,)))


def _dot_nt(a, b):
    return _dot3(a, b, ((1,), (1,)))


def _mask_dot(mask, x):
    m = mask.astype(BF16)
    return sum(_mxu(m, p, ((1,), (0,))) for p in _split(x, 3))


def _dot_mask(x, mask):
    m = mask.astype(BF16)
    return sum(_mxu(p, m, ((1,), (0,))) for p in _split(x, 3))


def _cumsum_maps(g, c):
    row = lax.broadcasted_iota(jnp.int32, (c, c), 0)
    col = lax.broadcasted_iota(jnp.int32, (c, c), 1)
    tril = row >= col
    gb = jnp.broadcast_to(g, (c, c))
    ri = _mask_dot(tril, gb)
    rj = _mask_dot(jnp.ones((c, c), BF16), jnp.where(row <= col, gb, 0.0))
    return ri, rj, tril, row, col


def _gdn_head(q, k, v, g, beta, s, c, levels):
    ri, rj, tril, row, col = _cumsum_maps(g, c)
    decay = jnp.where(tril, jnp.exp(jnp.where(tril, ri - rj, 0.0)), 0.0)
    gc = ri[:, 0:1]
    gc_last = ri[c - 1:c, 0:1]
    kb = k * beta
    m = jnp.where(row > col, _dot_nt(kb, k) * decay, 0.0)
    eye = (row == col).astype(F32)
    blk = (row // 8) == (col // 8)
    p = jnp.where(blk, m, 0.0)
    x = eye - p
    for _ in range(levels[0]):
        p = _dot(p, p)
        x = x + _dot(x, p)
    b = 8
    for _ in range(levels[1]):
        low = jnp.where(((row // (2 * b)) == (col // (2 * b))) & ((row // b) != (col // b)), m, 0.0)
        x = x - _dot(_dot(x, low), x)
        b *= 2
    u_in = _dot(x, v * beta)
    wk_in = _dot(x, kb * jnp.exp(gc))
    qk = _dot_nt(q, k) * decay
    q_dec = q * jnp.exp(gc)
    k_dec = k * jnp.exp(gc_last - gc)
    u = u_in - _dot(wk_in, s)
    o = _dot(q_dec, s) + _dot(qk, u)
    s_new = s * jnp.exp(gc_last) + _dot(k_dec.T, u)
    return o, s_new


def _gdn_kernel(q_ref, k_ref, v_ref, gb_ref, gate_ref, nw_ref, s0_ref, o_ref, sout_ref, s_scr,
                *, c, levels):
    ci = pl.program_id(1)

    @pl.when(ci == 0)
    def _():
        s_scr[...] = s0_ref[...]

    for hd in range(GDN_HEADS):
        sl = slice(hd * GDN_DV, (hd + 1) * GDN_DV)
        o, s_new = _gdn_head(q_ref[:, sl], k_ref[:, sl], v_ref[:, sl], gb_ref[:, hd:hd + 1],
                             gb_ref[:, GDN_HEADS + hd:GDN_HEADS + hd + 1], s_scr[hd], c, levels)
        s_scr[hd] = s_new
        sout_ref[hd] = s_new
        y = o * lax.rsqrt(jnp.mean(o * o, axis=-1, keepdims=True) + EPS) * nw_ref[...]
        gt = gate_ref[:, sl]
        o_ref[:, sl] = (y * (gt * jax.nn.sigmoid(gt))).astype(o_ref.dtype)


def _gdn_scan(qkv, gb, gate, gate_col, nw, s0, n_valid):
    b, l, _ = qkv.shape
    c = CHUNK
    n_eff = min(n_valid, c)
    base = min(n_eff, 8)
    levels = (max(0, math.ceil(math.log2(base)) - 1) if base > 1 else 0,
              max(0, math.ceil(math.log2(n_eff / 8))) if n_eff > 8 else 0)
    col = lambda j: pl.BlockSpec((None, c, MIX_W), lambda bi, ci: (bi, ci, j))
    st = pl.BlockSpec((None, GDN_HEADS, GDN_DK, GDN_DV), lambda bi, ci: (bi, 0, 0, 0))
    return pl.pallas_call(
        functools.partial(_gdn_kernel, c=c, levels=levels),
        grid=(b, l // c),
        in_specs=[col(0), col(1), col(2),
                  pl.BlockSpec((None, c, 8), lambda bi, ci: (bi, ci, 0)),
                  col(gate_col),
                  pl.BlockSpec((1, GDN_DV), lambda bi, ci: (0, 0)),
                  st],
        out_specs=[col(0), st],
        out_shape=[jax.ShapeDtypeStruct((b, l, MIX_W), BF16),
                   jax.ShapeDtypeStruct((b, GDN_HEADS, GDN_DK, GDN_DV), F32)],
        scratch_shapes=[pltpu.VMEM((GDN_HEADS, GDN_DK, GDN_DV), F32)],
        compiler_params=_cparams("parallel", "arbitrary"),
        name="gdn_scan",
    )(qkv, qkv, qkv, gb, gate, nw.reshape(1, GDN_DV), s0)


def _ssd_kernel(x_ref, da_ref, z_ref, dskip_ref, nw_ref, h0_ref, y_ref, hout_ref, h_scr, *, c):
    ci = pl.program_id(1)

    @pl.when(ci == 0)
    def _():
        h_scr[...] = h0_ref[...]

    lane = lax.broadcasted_iota(jnp.int32, (c, 128), 1)
    lane_lo = lane < 64
    hrow = lax.broadcasted_iota(jnp.int32, (128, 128), 0) < 64
    ys = []
    for pi in range(4):
        gmask = (lane_lo if pi < 2 else jnp.logical_not(lane_lo)).astype(F32)
        x = x_ref[:, pi * 128:(pi + 1) * 128]
        bm = x_ref[:, MIX_W:MIX_W + 128] * gmask
        cm = x_ref[:, MIX_W + 128:MIX_W + 256] * gmask
        hst = h_scr[pi]
        cb = _dot_nt(cm, bm)
        y = jnp.zeros((c, 128), F32)
        h_new = jnp.zeros((128, 128), F32)
        for hh in range(2):
            hd = 2 * pi + hh
            dt = da_ref[:, hd:hd + 1]
            da = da_ref[:, 8 + hd:9 + hd]
            ri, rj, tril, _, _ = _cumsum_maps(da, c)
            seg = jnp.where(tril, jnp.exp(jnp.where(tril, ri - rj, 0.0)), 0.0)
            a = ri[:, 0:1]
            a_last = ri[c - 1:c, 0:1]
            hm = lane_lo if hh == 0 else jnp.logical_not(lane_lo)
            xdt = jnp.where(hm, x, 0.0) * dt
            y_h = _dot(cb * seg, xdt) + _dot_nt(cm * jnp.exp(a), hst)
            y = y + jnp.where(hm, y_h, 0.0)
            upd = _dot(xdt.T, bm * jnp.exp(a_last - a))
            hsel = hrow if hh == 0 else jnp.logical_not(hrow)
            h_new = h_new + jnp.where(hsel, hst * jnp.exp(a_last) + upd, 0.0)
        h_scr[pi] = h_new
        hout_ref[pi] = h_new
        ys.append(y)
    y = jnp.concatenate(ys, axis=1) + dskip_ref[...] * x_ref[:, 0:MIX_W]
    z = z_ref[...]
    yz = y * (z * jax.nn.sigmoid(z))
    y_ref[...] = (yz * lax.rsqrt(jnp.mean(yz * yz, axis=-1, keepdims=True) + EPS)
                  * nw_ref[...]).astype(y_ref.dtype)


def _ssd_scan(xbc, da, z, z_col, dskip, nw, h0):
    b, l, w = xbc.shape
    c = CHUNK
    st = pl.BlockSpec((None, 4, 128, 128), lambda bi, ci: (bi, 0, 0, 0))
    vec = pl.BlockSpec((1, MIX_W), lambda bi, ci: (0, 0))
    return pl.pallas_call(
        functools.partial(_ssd_kernel, c=c),
        grid=(b, l // c),
        in_specs=[pl.BlockSpec((None, c, w), lambda bi, ci: (bi, ci, 0)),
                  pl.BlockSpec((None, c, 16), lambda bi, ci: (bi, ci, 0)),
                  pl.BlockSpec((None, c, MIX_W), lambda bi, ci: (bi, ci, z_col)),
                  vec, vec, st],
        out_specs=[pl.BlockSpec((None, c, MIX_W), lambda bi, ci: (bi, ci, 0)), st],
        out_shape=[jax.ShapeDtypeStruct((b, l, MIX_W), BF16),
                   jax.ShapeDtypeStruct((b, 4, 128, 128), F32)],
        scratch_shapes=[pltpu.VMEM((4, 128, 128), F32)],
        compiler_params=_cparams("parallel", "arbitrary"),
        name="ssd_scan",
    )(xbc, da, z, dskip.reshape(1, MIX_W), nw.reshape(1, MIX_W), h0)


def _fox_prompt_kernel(q_ref, k_ref, v_ref, cq_ref, ck_ref, o_ref, kb_scr, vb_scr,
                       m_scr, l_scr, acc_scr, qh_scr, cq_scr, *, tq, tk):
    i = pl.program_id(2)

    @pl.when(i == 0)
    def _():
        kb_scr[...] = k_ref[...].astype(BF16)
        vb_scr[...] = v_ref[...].astype(BF16)

    lane = lax.broadcasted_iota(jnp.int32, (tq, 128), 1)
    q = q_ref[...] * (FOX_SCALE * LOG2E)
    rep = tk // 128
    for hh in range(2):
        hm = (lane < 64) if hh == 0 else (lane >= 64)
        qh_scr[hh] = jnp.where(hm, q, 0.0).astype(BF16)
        cq_scr[hh] = jnp.concatenate([jnp.broadcast_to(cq_ref[:, hh:hh + 1], (tq, 128))] * rep, axis=1)
        m_scr[hh] = jnp.full((tq, 128), -jnp.inf, F32)
        l_scr[hh] = jnp.zeros((tq, 128), F32)
        acc_scr[hh] = jnp.zeros((tq, 128), F32)

    def step(j, masked):
        off = pl.multiple_of(j * tk, tk)
        kk = kb_scr[pl.ds(off, tk), :]
        vv = vb_scr[pl.ds(off, tk), :]
        for hh in range(2):
            s = lax.dot_general(qh_scr[hh], kk, (((1,), (1,)), ((), ())),
                                preferred_element_type=F32)
            s = s + cq_scr[hh] - ck_ref[j, hh:hh + 1, :]
            if masked:
                rowp = lax.broadcasted_iota(jnp.int32, (tq, tk), 0)
                colp = lax.broadcasted_iota(jnp.int32, (tq, tk), 1)
                s = jnp.where(colp <= rowp, s, -jnp.inf)
            m_prev = m_scr[hh]
            m_new = jnp.maximum(m_prev, jnp.max(s, axis=1, keepdims=True))
            alpha = jnp.exp2(m_prev - m_new)
            p = jnp.exp2(s - jnp.concatenate([m_new] * rep, axis=1))
            l_scr[hh] = alpha * l_scr[hh] + jnp.sum(p, axis=1, keepdims=True)
            acc_scr[hh] = alpha * acc_scr[hh] + jnp.dot(p.astype(BF16), vv,
                                                        preferred_element_type=F32)
            m_scr[hh] = m_new

    def body(j, carry):
        step(j, False)
        return carry

    lax.fori_loop(0, i, body, 0)
    step(i, True)
    o_ref[...] = jnp.where(lane < 64, acc_scr[0] / l_scr[0], acc_scr[1] / l_scr[1]).astype(o_ref.dtype)


def _fox_prompt(proj, cum, bsz, seq):
    tq = tk = 512 if seq % 512 == 0 else seq
    nq = seq // tq
    cum = cum * LOG2E
    cq = cum.reshape(bsz, seq, 4, 2).transpose(0, 2, 1, 3)
    ck = cum.reshape(bsz, nq, tk, 4, 2).transpose(0, 3, 1, 4, 2)
    qb = C_BQKV // 128
    kb = (C_BQKV + 512) // 128
    vb = (C_BQKV + 1024) // 128
    return pl.pallas_call(
        functools.partial(_fox_prompt_kernel, tq=tq, tk=tk),
        grid=(bsz, 4, nq),
        in_specs=[pl.BlockSpec((tq, 128), lambda b, p, i: (b * nq + i, qb + p)),
                  pl.BlockSpec((seq, 128), lambda b, p, i: (b, kb + p)),
                  pl.BlockSpec((seq, 128), lambda b, p, i: (b, vb + p)),
                  pl.BlockSpec((None, None, tq, 2), lambda b, p, i: (b, p, i, 0)),
                  pl.BlockSpec((None, None, nq, 2, tk), lambda b, p, i: (b, p, 0, 0, 0))],
        out_specs=pl.BlockSpec((tq, 128), lambda b, p, i: (b * nq + i, p)),
        out_shape=jax.ShapeDtypeStruct((bsz * seq, 512), BF16),
        scratch_shapes=[pltpu.VMEM((seq, 128), BF16), pltpu.VMEM((seq, 128), BF16),
                        pltpu.VMEM((2, tq, 128), F32), pltpu.VMEM((2, tq, 128), F32),
                        pltpu.VMEM((2, tq, 128), F32), pltpu.VMEM((2, tq, 128), BF16),
                        pltpu.VMEM((2, tq, tk), F32)],
        compiler_params=_cparams("parallel", "parallel", "arbitrary"),
        name="fox_prompt",
    )(proj, proj, proj, cq, ck)


def _fox_sample_kernel(pt_ref, q_ref, kn_ref, vn_ref, ln_ref, *rest, n_new, pps):
    page_refs = [rest[3 * i:3 * i + 3] for i in range(pps)]
    o_ref, m_scr, l_scr, acc_scr, cs_scr, suf_scr = rest[3 * pps:]
    s_id = pl.program_id(1)
    n_steps = pl.num_programs(1)
    r = 8 * n_new
    rowi = lax.broadcasted_iota(jnp.int32, (r, MIX_W), 0)
    lanei = lax.broadcasted_iota(jnp.int32, (r, MIX_W), 1)
    hmask = (lanei // FOX_DH) == (rowi % 8)
    q = q_ref[...] * FOX_SCALE
    qrep = jnp.concatenate([jnp.broadcast_to(q[t:t + 1, :], (8, MIX_W)) for t in range(n_new)], axis=0)
    qbd = jnp.where(hmask, qrep, 0.0).astype(BF16)
    krow = lax.broadcasted_iota(jnp.int32, (128, 128), 0)
    kcol = lax.broadcasted_iota(jnp.int32, (128, 128), 1)
    rr = lax.broadcasted_iota(jnp.int32, (r, 128), 0)
    kk = lax.broadcasted_iota(jnp.int32, (r, 128), 1)

    def attend(parts):
        ss = []
        for kx, _, bias, valid in parts:
            s = jnp.dot(qbd, kx.astype(BF16), preferred_element_type=F32) + bias
            ss.append(s if valid is None else jnp.where(valid, s, -jnp.inf))
        m_new = m_prev = m_scr[...]
        for s in ss:
            m_new = jnp.maximum(m_new, jnp.max(s, axis=1, keepdims=True))
        alpha = jnp.exp(m_prev - m_new)
        l_new = alpha * l_scr[...]
        acc = jnp.concatenate([alpha] * (MIX_W // 128), axis=1) * acc_scr[...]
        for (_, vx, _, _), s in zip(parts, ss):
            p = jnp.exp(s - m_new)
            l_new = l_new + jnp.sum(p, axis=1, keepdims=True)
            acc = acc + lax.dot_general(p.astype(BF16), vx.astype(BF16), (((1,), (1,)), ((), ())),
                                        preferred_element_type=F32)
        l_scr[...] = l_new
        acc_scr[...] = acc
        m_scr[...] = m_new

    @pl.when(s_id == 0)
    def _():
        m_scr[...] = jnp.full((r, 128), -jnp.inf, F32)
        l_scr[...] = jnp.zeros((r, 128), F32)
        acc_scr[...] = jnp.zeros((r, MIX_W), F32)
        suf_scr[...] = jnp.zeros((r, 128), F32)
        lf = jnp.concatenate([ln_ref[...]] * n_new, axis=0)
        cum = _dot_mask(lf, krow <= kcol)
        tq = rr // 8
        cs = jnp.sum(jnp.where(kk == tq, cum, 0.0), axis=1, keepdims=True)
        cs_scr[...] = jnp.broadcast_to(cs, (r, 128))
        attend([(kn_ref[...], vn_ref[...], cs - cum, kk <= tq)])

    @pl.when(s_id > 0)
    def _():
        suf = suf_scr[...]
        cs = cs_scr[...]
        parts = []
        for kp_ref, vp_ref, lp_ref in page_refs:
            lf = jnp.concatenate([lp_ref[...]] * n_new, axis=0)
            after = _dot_mask(lf, krow > kcol)
            parts.append((kp_ref[...].reshape(MIX_W, PAGE_SIZE), vp_ref[...].reshape(MIX_W, PAGE_SIZE),
                          cs + suf + after, None))
            suf = suf + jnp.sum(lf, axis=1, keepdims=True)
        attend(parts)
        suf_scr[...] = suf

    @pl.when(s_id == n_steps - 1)
    def _():
        o = jnp.where(hmask, acc_scr[...] / jnp.concatenate([l_scr[...]] * (MIX_W // 128), axis=1), 0.0)
        o_ref[...] = jnp.sum(o.reshape(n_new, 8, MIX_W), axis=1)


def _fox_sample(q, k_new, v_new, logf_new, cache_k, cache_v, cache_lf, page_table, layer):
    bd, n_new, _ = q.shape
    n_pages = page_table.shape[1]
    r = 8 * n_new
    key_last = lambda a: _pad_seq(a, 1, PAGE_SIZE).transpose(0, 2, 1)
    pps = 2 if n_pages % 2 == 0 else 1

    def page(i):
        return lambda b, s, pt: pt[b, n_pages - 1 - ((jnp.maximum(s, 1) - 1) * pps + i)]

    per_b = lambda shape: pl.BlockSpec((None,) + shape, lambda b, s, pt: (b, 0, 0))
    page_specs, page_args = [], []
    for i in range(pps):
        pg = page(i)
        kv = pl.BlockSpec((None, None, FOX_HEADS, FOX_DH, PAGE_SIZE),
                          lambda b, s, pt, pg=pg: (layer, pg(b, s, pt), 0, 0, 0))
        lf = pl.BlockSpec((None, None, FOX_HEADS, PAGE_SIZE),
                          lambda b, s, pt, pg=pg: (layer, pg(b, s, pt), 0, 0))
        page_specs += [kv, kv, lf]
        page_args += [cache_k, cache_v, cache_lf]
    grid_spec = pltpu.PrefetchScalarGridSpec(
        num_scalar_prefetch=1,
        grid=(bd, n_pages // pps + 1),
        in_specs=[per_b((n_new, MIX_W)), per_b((MIX_W, PAGE_SIZE)), per_b((MIX_W, PAGE_SIZE)),
                  per_b((FOX_HEADS, PAGE_SIZE))] + page_specs,
        out_specs=per_b((n_new, MIX_W)),
        scratch_shapes=[pltpu.VMEM((r, 128), F32), pltpu.VMEM((r, 128), F32),
                        pltpu.VMEM((r, MIX_W), F32), pltpu.VMEM((r, 128), F32),
                        pltpu.VMEM((r, 128), F32)])
    o = pl.pallas_call(
        functools.partial(_fox_sample_kernel, n_new=n_new, pps=pps),
        grid_spec=grid_spec,
        out_shape=jax.ShapeDtypeStruct((bd, n_new, MIX_W), F32),
        compiler_params=_cparams("parallel", "arbitrary"),
        name="fox_sample",
    )(page_table, q, key_last(k_new), key_last(v_new), key_last(logf_new), *page_args)
    return o.reshape(bd * n_new, MIX_W)


def _merge_kernel(oa_ref, ob_ref, oc_ref, g0_ref, g1_ref, g2_ref, wb_ref, wo_ref, x_ref,
                  gate_ref, nw_ref, sc_ref, sh_ref, xo_ref, h_ref):
    acc = None
    for br, (o_r, g_r) in enumerate(((oa_ref, g0_ref), (ob_ref, g1_ref), (oc_ref, g2_ref))):
        y = jnp.dot(o_r[...].astype(BF16), wb_ref[br], preferred_element_type=F32)
        y = jax.nn.sigmoid(g_r[...]) * y
        acc = y if acc is None else acc + y
    out = jnp.dot(acc.astype(BF16), wo_ref[...], preferred_element_type=F32)
    xn = x_ref[...] + gate_ref[...] * out
    xo_ref[...] = xn
    h_ref[...] = _rms_mod(xn, nw_ref[...], sc_ref[...], sh_ref[...])


def _merge(o_a, o_b, o_c, proj, wb, wo, x, g1, nw, sc2, sh2, seq_len):
    t, d = x.shape
    tm = min(256, t)
    g_a, g_s = _mod_operand(g1, seq_len, tm)
    sc_a, sc_s = _mod_operand(sc2, seq_len, tm)
    sh_a, sh_s = _mod_operand(sh2, seq_len, tm)
    br = pl.BlockSpec((tm, MIX_W), lambda i: (i, 0))
    row = pl.BlockSpec((tm, d), lambda i: (i, 0))
    return pl.pallas_call(
        _merge_kernel,
        grid=(t // tm,),
        in_specs=[br, br, br,
                  pl.BlockSpec((tm, d), lambda i: (i, 0)),
                  pl.BlockSpec((tm, d), lambda i: (i, 1)),
                  pl.BlockSpec((tm, d), lambda i: (i, 2)),
                  pl.BlockSpec((3, MIX_W, d), lambda i: (0, 0, 0)),
                  pl.BlockSpec((d, d), lambda i: (0, 0)),
                  row, g_s,
                  pl.BlockSpec((1, d), lambda i: (0, 0)),
                  sc_s, sh_s],
        out_specs=[row, row],
        out_shape=[jax.ShapeDtypeStruct((t, d), F32), jax.ShapeDtypeStruct((t, d), F32)],
        compiler_params=_cparams("parallel"),
        name="merge",
    )(o_a, o_b, o_c, proj, proj, proj, wb, wo, x, g_a, nw.reshape(1, d), sc_a, sh_a)


def _top16_rows(s):
    rows, tb = s.shape
    rid = lax.broadcasted_iota(jnp.int32, (rows, tb), 0).astype(F32)
    out_row = lax.broadcasted_iota(jnp.int32, (PEER_TOPK, tb), 0)
    vals = jnp.zeros((PEER_TOPK, tb), F32)
    idxs = jnp.zeros((PEER_TOPK, tb), F32)
    for it in range(PEER_TOPK):
        mx = jnp.max(s, axis=0, keepdims=True)
        ix = jnp.min(jnp.where(s == mx, rid, float(rows)), axis=0, keepdims=True)
        vals = jnp.where(out_row == it, mx, vals)
        idxs = jnp.where(out_row == it, ix, idxs)
        s = jnp.where(rid == ix, -jnp.inf, s)
    return vals, idxs


_CAND_GROUPS = (("i", 0, 0, 16), ("j", 0, 1, 16), ("i", 1, 1, 8), ("j", 1, 2, 8),
                ("i", 2, 2, 5), ("i", 3, 2, 4), ("i", 4, 2, 3))


def _candidates(a, b, ia, ib):
    tb = a.shape[1]
    vals, poss, cids = [], [], []
    for kind, fixed, lo, hi in _CAND_GROUPS:
        n = 16 if hi > 8 else 8
        r = lax.broadcasted_iota(jnp.int32, (n, tb), 0)
        ok = (r >= lo) & (r < hi)
        if kind == "i":
            v = a[fixed:fixed + 1] + b[0:n]
            c = ia[fixed:fixed + 1] * float(PEER_NKEYS) + ib[0:n]
            p = fixed * PEER_TOPK + r
        else:
            v = a[0:n] + b[fixed:fixed + 1]
            c = ia[0:n] * float(PEER_NKEYS) + ib[fixed:fixed + 1]
            p = r * PEER_TOPK + fixed
        vals.append(jnp.where(ok, v, -jnp.inf))
        poss.append(jnp.where(ok, p, 4096).astype(F32))
        cids.append(c)
    return (jnp.concatenate(vals, axis=0), jnp.concatenate(poss, axis=0),
            jnp.concatenate(cids, axis=0))


def _route_kernel(h_ref, wq_ref, keys_ref, eid_ref, gate_ref, *, tb):
    qt = lax.dot_general(wq_ref[...], h_ref[...].astype(BF16), (((1,), (1,)), ((), ())),
                         preferred_element_type=F32)
    out_row = lax.broadcasted_iota(jnp.int32, (PEER_TOPK, tb), 0)
    tops, eids = [], []
    for hd in range(PEER_HEADS):
        sv, si = [], []
        for half in range(2):
            c0 = (hd * 2 + half) * 128
            sc = jnp.dot(keys_ref[half], qt[c0:c0 + 128, :].astype(BF16),
                         preferred_element_type=F32)
            v_, i_ = _top16_rows(sc)
            sv.append(v_)
            si.append(i_)
        cand, cpos, cid = _candidates(sv[0], sv[1], si[0], si[1])
        top = jnp.zeros((PEER_TOPK, tb), F32)
        eid = jnp.zeros((PEER_TOPK, tb), F32)
        for it in range(PEER_TOPK):
            mx = jnp.max(cand, axis=0, keepdims=True)
            px = jnp.min(jnp.where(cand == mx, cpos, 8192.0), axis=0, keepdims=True)
            hit = cpos == px
            ev = jnp.sum(jnp.where(hit, cid, 0.0), axis=0, keepdims=True)
            top = jnp.where(out_row == it, mx, top)
            eid = jnp.where(out_row == it, ev, eid)
            cand = jnp.where(hit, -jnp.inf, cand)
        e = jnp.exp(top - top[0:1])
        tops.append(e / jnp.sum(e, axis=0, keepdims=True))
        eids.append(eid)
    gate_ref[...] = jnp.concatenate(tops, axis=0)
    eid_ref[...] = jnp.concatenate(eids, axis=0).T.astype(jnp.int32)


def _route(h, wq_t, keys):
    t, d = h.shape
    tb = 128
    return pl.pallas_call(
        functools.partial(_route_kernel, tb=tb),
        grid=(t // tb,),
        in_specs=[pl.BlockSpec((tb, d), lambda i: (i, 0)),
                  pl.BlockSpec((PEER_HEADS * PEER_DQ, d), lambda i: (0, 0)),
                  pl.BlockSpec((2, 128, 128), lambda i: (0, 0, 0))],
        out_specs=[pl.BlockSpec((tb, 128), lambda i: (i, 0)),
                   pl.BlockSpec((None, 128, tb), lambda i: (i, 0, 0))],
        out_shape=[jax.ShapeDtypeStruct((t, 128), jnp.int32),
                   jax.ShapeDtypeStruct((t // tb, 128, tb), F32)],
        compiler_params=_cparams("parallel"),
        name="peer_route",
    )(h, wq_t, keys)


GATHER_WINDOW = 64
PEER_CHUNKS = 4


def _sc_gather(table, idx):
    n = idx.shape[0]
    width = table.shape[1]
    mesh = plsc.VectorSubcoreMesh(core_axis_name="core", subcore_axis_name="subcore")
    n_workers = mesh.num_cores * mesh.num_subcores
    win = GATHER_WINDOW
    per_w = n // n_workers
    n_chunks = per_w // win
    assert per_w * n_workers == n and n_chunks * win == per_w and n_chunks % 2 == 0

    @pl.kernel(out_type=jax.ShapeDtypeStruct((n, width), table.dtype), mesh=mesh,
               scratch_types=[pltpu.VMEM((win,), jnp.int32), pltpu.VMEM((win,), jnp.int32),
                              pltpu.VMEM((win, width), table.dtype),
                              pltpu.VMEM((win, width), table.dtype),
                              pltpu.SemaphoreType.DMA, pltpu.SemaphoreType.DMA,
                              pltpu.SemaphoreType.DMA, pltpu.SemaphoreType.DMA],
               name="peer_gather")
    def gather_kernel(tab_hbm, idx_hbm, out_hbm, i0, i1, r0, r1, g0, g1, w0, w1):
        wid = lax.axis_index("subcore") * mesh.num_cores + lax.axis_index("core")
        base = wid * per_w
        bufs = ((i0, r0, g0, w0), (i1, r1, g1, w1))

        def write_copy(rv, ws, off):
            return pltpu.make_async_copy(rv, out_hbm.at[pl.ds(off, win)], ws)

        @pl.loop(0, n_chunks, step=2)
        def _(c):
            for b, (iv, rv, gs, ws) in enumerate(bufs):
                off = base + (c + b) * win

                @pl.when(c > 0)
                def _():
                    write_copy(rv, ws, off).wait()

                pltpu.sync_copy(idx_hbm.at[pl.ds(off, win)], iv)
                pltpu.make_async_copy(tab_hbm.at[iv], rv, gs).start()
            for b, (iv, rv, gs, ws) in enumerate(bufs):
                off = base + (c + b) * win
                pltpu.make_async_copy(tab_hbm.at[iv], rv, gs).wait()
                write_copy(rv, ws, off).start()

        for iv, rv, gs, ws in bufs:
            write_copy(rv, ws, base).wait()

    return gather_kernel(table, idx)


def _pack_table(tab):
    b = lax.bitcast_convert_type(tab.astype(BF16), jnp.uint16).astype(jnp.uint32)
    return lax.bitcast_convert_type(b[:, :512] | (b[:, 512:] << 16), jnp.int32)


def _unpack(w):
    lo = pltpu.bitcast(w << 16, F32)
    hi = pltpu.bitcast(w & jnp.int32(-65536), F32)
    return lo, hi


def _peer_kernel(gu_ref, gv_ref, h_ref, gt_ref, x_ref, g2_ref, o_ref, *, tb):
    base = (pl.program_id(0) * tb) % 128
    gt = gt_ref[...]
    tok = lax.broadcasted_iota(jnp.int32, (128, 128), 1)
    for t in range(tb):
        h_lo = h_ref[t:t + 1, 0:512]
        h_hi = h_ref[t:t + 1, 512:1024]
        u_lo, u_hi = _unpack(gu_ref[t * 128:(t + 1) * 128, :])
        pu = u_lo * h_lo + u_hi * h_hi
        p128 = pu[:, 0:128] + pu[:, 128:256] + pu[:, 256:384] + pu[:, 384:512]
        pre = jnp.sum(p128, axis=1, keepdims=True)
        act = 0.5 * pre * (1.0 + lax.erf(pre * (2.0 ** -0.5)))
        gcol = jnp.sum(jnp.where(tok == base + t, gt, 0.0), axis=1, keepdims=True)
        w = gcol * act
        v_lo, v_hi = _unpack(gv_ref[t * 128:(t + 1) * 128, :])
        y_lo = jnp.sum(v_lo * w, axis=0, keepdims=True)
        y_hi = jnp.sum(v_hi * w, axis=0, keepdims=True)
        o_ref[t:t + 1, 0:512] = x_ref[t:t + 1, 0:512] + g2_ref[t:t + 1, 0:512] * y_lo
        o_ref[t:t + 1, 512:1024] = x_ref[t:t + 1, 512:1024] + g2_ref[t:t + 1, 512:1024] * y_hi


def _peer_mix(gu, gv, h, gate_t, x, g2, seq_len, tok0=0):
    d = x.shape[1]
    n = gu.shape[0] // 128
    tb = 16
    assert tok0 % 128 == 0
    b0 = tok0 // tb
    if seq_len % tb == 0:
        per = seq_len // tb
        g_a = jnp.broadcast_to(g2.reshape(g2.shape[0], 1, d), (g2.shape[0], tb, d))
        g_s = pl.BlockSpec((None, tb, d), lambda i: ((i + b0) // per, 0, 0))
    else:
        g_a = jnp.repeat(g2, seq_len, axis=0)
        g_s = pl.BlockSpec((tb, d), lambda i: (i + b0, 0))
    row = pl.BlockSpec((tb, d), lambda i: (i + b0, 0))
    rows = pl.BlockSpec((tb * 128, 512), lambda i: (i, 0))
    per_grp = 128 // tb
    return pl.pallas_call(
        functools.partial(_peer_kernel, tb=tb),
        grid=(n // tb,),
        in_specs=[rows, rows, row,
                  pl.BlockSpec((None, 128, 128), lambda i: ((i + b0) // per_grp, 0, 0)),
                  row, g_s],
        out_specs=pl.BlockSpec((tb, d), lambda i: (i, 0)),
        out_shape=jax.ShapeDtypeStruct((n, d), F32),
        compiler_params=_cparams("parallel"),
        name="peer_mix",
    )(gu, gv, h, gate_t, x, g_a)


def _final_norm_kernel(x_ref, w_ref, o_ref):
    x = x_ref[...]
    o_ref[...] = x * lax.rsqrt(jnp.mean(x * x, axis=-1, keepdims=True) + EPS) * w_ref[...]


def _final_norm(x, w):
    t, d = x.shape
    tm = min(512, t)
    return pl.pallas_call(
        _final_norm_kernel,
        grid=(t // tm,),
        in_specs=[pl.BlockSpec((tm, d), lambda i: (i, 0)), pl.BlockSpec((1, d), lambda i: (0, 0))],
        out_specs=pl.BlockSpec((tm, d), lambda i: (i, 0)),
        out_shape=jax.ShapeDtypeStruct((t, d), F32),
        compiler_params=_cparams("parallel"),
        name="final_norm",
    )(x, w.reshape(1, d))


def _causal_conv(x, buf, w):
    l = x.shape[1]
    xp = jnp.concatenate([buf.astype(x.dtype), x], axis=1)
    y = sum(xp[:, i:i + l] * w[i] for i in range(CONV_W))
    return y, xp[:, l:]


def _l2norm(x):
    return x * lax.rsqrt(jnp.sum(x * x, axis=-1, keepdims=True) + EPS)


def _pad_seq(a, axis, lp):
    pad = lp - a.shape[axis]
    if pad == 0:
        return a
    cfg = [(0, 0)] * a.ndim
    cfg[axis] = (0, pad)
    return jnp.pad(a, cfg)


def _layer(x, mod, lw, states, bsz, seq, fox):
    sh1, sc1, g1, sh2, sc2, g2 = mod
    gdn_s, gdn_b, ssm_s, ssm_b = states
    t = bsz * seq
    lp = -(-seq // CHUNK) * CHUNK
    proj = _proj_in(x, lw["norm_mix"], sc1, sh1, lw["w_in"], seq)
    p3 = proj.reshape(bsz, seq, N_PROJ)
    small = p3[:, :, C_SMALL:C_SMALL + 24]
    a_beta, a_alpha, b_f, c_dt = small[..., 0:4], small[..., 4:8], small[..., 8:16], small[..., 16:24]

    a_conv, gdn_b_new = _causal_conv(p3[:, :, C_AQKV:C_AQKV + GDN_CONV_CH], gdn_b, lw["gdn_conv_w"])
    a_act = jax.nn.silu(a_conv)
    q = _l2norm(a_act[..., 0:512].reshape(bsz, seq, GDN_HEADS, GDN_DK)) * GDN_DK ** -0.5
    k = _l2norm(a_act[..., 512:1024].reshape(bsz, seq, GDN_HEADS, GDN_DK))
    qkv = jnp.concatenate([q.reshape(bsz, seq, MIX_W), k.reshape(bsz, seq, MIX_W),
                           a_act[..., 1024:1536]], axis=-1)
    beta = jax.nn.sigmoid(a_beta)
    g = -jnp.exp(lw["gdn_a_log"]) * jax.nn.softplus(a_alpha + lw["gdn_dt_bias"])
    gb = jnp.concatenate([g, beta], axis=-1)
    if seq % CHUNK == 0:
        gate_arr, gate_col = p3, C_AGATE // MIX_W
    else:
        gate_arr, gate_col = _pad_seq(p3[:, :, C_AGATE:C_AGATE + MIX_W], 1, lp), 0
    o_a, gdn_s_new = _gdn_scan(_pad_seq(qkv, 1, lp), _pad_seq(gb, 1, lp), gate_arr, gate_col,
                               lw["gdn_norm_w"], gdn_s, seq)
    o_a = o_a[:, :seq].reshape(t, MIX_W)

    logf = jax.nn.log_sigmoid(b_f + lw["fox_f_bias"])
    kf = p3[:, :, C_BQKV + 512:C_BQKV + 1024].reshape(bsz, seq, FOX_HEADS, FOX_DH)
    vf = p3[:, :, C_BQKV + 1024:C_BQKV + 1536].reshape(bsz, seq, FOX_HEADS, FOX_DH)
    o_b = fox(proj, p3, logf)

    xbc, ssm_b_new = _causal_conv(p3[:, :, C_CXBC:C_CXBC + SSM_CONV_CH], ssm_b, lw["ssm_conv_w"])
    xbc = jax.nn.silu(xbc + lw["ssm_conv_b"])
    dt = jax.nn.softplus(c_dt + lw["ssm_dt_bias"])
    da = jnp.concatenate([dt, dt * (-jnp.exp(lw["ssm_a_log"]))], axis=-1)
    if seq % CHUNK == 0:
        z_arr, z_col = p3, C_CZ // MIX_W
    else:
        z_arr, z_col = _pad_seq(p3[:, :, C_CZ:C_CZ + MIX_W], 1, lp), 0
    o_c, ssm_s_new = _ssd_scan(_pad_seq(xbc, 1, lp), _pad_seq(da, 1, lp), z_arr, z_col,
                               jnp.repeat(lw["ssm_d"], SSM_P), lw["ssm_norm_w"], ssm_s)
    o_c = o_c[:, :seq].reshape(t, MIX_W)

    x, h2 = _merge(o_a, o_b, o_c, proj, lw["w_branch"], lw["w_out"], x, g1, lw["norm_ffn"],
                   sc2, sh2, seq)
    eid, gate_t = _route(h2, lw["peer_wq_t"], lw["peer_keys"])
    n_chunks = PEER_CHUNKS if t % (PEER_CHUNKS * 1024) == 0 else 1
    tc = t // n_chunks
    parts = []
    for c in range(n_chunks):
        flat = eid[c * tc:(c + 1) * tc].reshape(-1)
        gu = _sc_gather(lw["peer_u"], flat)
        gv = _sc_gather(lw["peer_v"], flat)
        parts.append(_peer_mix(gu, gv, h2, gate_t, x, g2, seq, c * tc))
    x = parts[0] if n_chunks == 1 else jnp.concatenate(parts, axis=0)
    return x, (kf, vf, logf, gdn_s_new, gdn_b_new, ssm_s_new, ssm_b_new)


def _ssm_state_pack(s):
    b = s.shape[0]
    sp = s.reshape(b, 4, 128, SSM_N)
    z = jnp.zeros_like(sp)
    lo = jnp.concatenate([sp, z], axis=-1)
    hi = jnp.concatenate([z, sp], axis=-1)
    grp = (jnp.arange(4) // 2).reshape(1, 4, 1, 1)
    return jnp.where(grp == 0, lo, hi)


def _ssm_state_unpack(sp):
    b = sp.shape[0]
    out = jnp.where((jnp.arange(4) // 2).reshape(1, 4, 1, 1) == 0, sp[..., :SSM_N], sp[..., SSM_N:])
    return out.reshape(b, SSM_HEADS, SSM_P, SSM_N)


def kernel(x_prompt, x_sample, cache_k, cache_v, cache_logf, page_table, state_gdn, state_gdn_conv,
           state_ssm, state_ssm_conv, c_prompt, c_sample, w_ada, b_ada, norm_mix, norm_ffn, w_in,
           gdn_conv_w, gdn_a_log, gdn_dt_bias, gdn_norm_w, fox_f_bias, ssm_conv_w, ssm_conv_b,
           ssm_a_log, ssm_dt_bias, ssm_d, ssm_norm_w, w_branch, w_out, peer_wq, peer_keys, peer_u,
           peer_v, final_norm):
    n_pr, seq, d = x_prompt.shape
    n_dec, dec_seq, _ = x_sample.shape
    n_pool = cache_k.shape[1]
    xp = x_prompt.reshape(n_pr * seq, d)
    xs = x_sample.reshape(n_dec * dec_seq, d)
    c_all = jnp.concatenate([c_prompt, c_sample], axis=0)
    n_c = c_all.shape[0]
    c_all = jnp.pad(c_all, ((0, -n_c % 8), (0, 0)))
    cache_kt = cache_k.transpose(0, 1, 3, 4, 2)
    cache_vt = cache_v.transpose(0, 1, 3, 4, 2)
    cache_lf = cache_logf.transpose(0, 1, 3, 2)
    new_p, new_s = [], []
    for l in range(DEPTH):
        lw = {"norm_mix": norm_mix[l], "norm_ffn": norm_ffn[l], "w_in": _pack_w_in(w_in[l]),
              "gdn_conv_w": gdn_conv_w[l], "gdn_a_log": gdn_a_log[l], "gdn_dt_bias": gdn_dt_bias[l],
              "gdn_norm_w": gdn_norm_w[l], "fox_f_bias": fox_f_bias[l], "ssm_conv_w": ssm_conv_w[l],
              "ssm_conv_b": ssm_conv_b[l], "ssm_a_log": ssm_a_log[l], "ssm_dt_bias": ssm_dt_bias[l],
              "ssm_d": ssm_d[l], "ssm_norm_w": ssm_norm_w[l], "w_branch": w_branch[l].astype(BF16),
              "w_out": w_out[l].astype(BF16), "peer_wq_t": peer_wq[l].T.astype(BF16),
              "peer_keys": peer_keys[l].astype(BF16),
              "peer_u": _pack_table(peer_u[l]), "peer_v": _pack_table(peer_v[l])}
        mod = _adaln(c_all, w_ada[l].astype(BF16), b_ada[l])
        mod_p = jnp.split(mod[:n_pr], 6, axis=-1)
        mod_s = jnp.split(mod[n_pr:n_c], 6, axis=-1)

        def fox_p(proj, p3, logf):
            return _fox_prompt(proj, jnp.cumsum(logf, axis=1), n_pr, seq)

        zero_states = (jnp.zeros((n_pr, GDN_HEADS, GDN_DK, GDN_DV), F32),
                       jnp.zeros((n_pr, CONV_W - 1, GDN_CONV_CH), F32),
                       jnp.zeros((n_pr, 4, 128, 128), F32),
                       jnp.zeros((n_pr, CONV_W - 1, SSM_CONV_CH), F32))
        xp, st = _layer(xp, mod_p, lw, zero_states, n_pr, seq, fox_p)
        new_p.append(st[:5] + (_ssm_state_unpack(st[5]), st[6]))

        def fox_s(proj, p3, logf, l=l):
            qkv = [p3[:, :, C_BQKV + i * MIX_W:C_BQKV + (i + 1) * MIX_W] for i in range(3)]
            return _fox_sample(*qkv, logf, cache_kt, cache_vt, cache_lf, page_table, l).astype(BF16)

        s_states = (state_gdn[l], state_gdn_conv[l], _ssm_state_pack(state_ssm[l]), state_ssm_conv[l])
        xs, st = _layer(xs, mod_s, lw, s_states, n_dec, dec_seq, fox_s)
        new_s.append(st[:5] + (_ssm_state_unpack(st[5]), st[6]))
    y_prompt = _final_norm(xp, final_norm).reshape(n_pr, seq, d)
    y_sample = _final_norm(xs, final_norm).reshape(n_dec, dec_seq, d)
    k_p, v_p, f_p, g_p, gc_p, s_p, sc_p = [jnp.stack(z) for z in zip(*new_p)]
    k_s, v_s, f_s, g_s, gc_s, s_s, sc_s = [jnp.stack(z) for z in zip(*new_s)]
    return (y_prompt, y_sample, k_p, v_p, f_p, g_p, gc_p, s_p, sc_p,
            k_s, v_s, f_s, g_s, gc_s, s_s, sc_s)
```

```python
import functools
import math

import jax
import jax.numpy as jnp
from jax import lax
from jax.experimental import pallas as pl
from jax.experimental.pallas import tpu as pltpu
from jax.experimental.pallas import tpu_sc as plsc

F32 = jnp.float32
BF16 = jnp.bfloat16

D_MODEL = 1024
DEPTH = 4
PAGE_SIZE = 128
MIX_W = D_MODEL // 2
CONV_W = 4
GDN_DK = 128
GDN_DV = 128
GDN_HEADS = MIX_W // GDN_DV
FOX_DH = 64
FOX_HEADS = MIX_W // FOX_DH
FOX_SCALE = FOX_DH ** -0.5
SSM_P = 64
SSM_HEADS = MIX_W // SSM_P
SSM_GROUPS = 2
SSM_N = 64
PEER_HEADS = 8
PEER_NKEYS = 128
PEER_DQ = 256
PEER_TOPK = 16
EPS = 1e-6

GDN_CONV_CH = 2 * GDN_HEADS * GDN_DK + GDN_HEADS * GDN_DV
SSM_CONV_CH = MIX_W + 2 * SSM_GROUPS * SSM_N
IN_SPLITS = (GDN_CONV_CH, MIX_W, GDN_HEADS, GDN_HEADS, 3 * MIX_W, FOX_HEADS,
             MIX_W, SSM_CONV_CH, SSM_HEADS, 3 * D_MODEL)

C_GATES = 0
C_AQKV = 3072
C_AGATE = 4608
C_BQKV = 5120
C_CZ = 6656
C_CXBC = 7168
C_SMALL = 7936
N_PROJ = 8192

LOG2E = 1.4426950408889634
CHUNK = 128
SHORT_CHUNK = 16
VMEM_LIMIT = 48 * 1024 * 1024


def _cparams(*sem):
    return pltpu.CompilerParams(dimension_semantics=sem, vmem_limit_bytes=VMEM_LIMIT)


def _pack_w_in(w_in):
    offs = [0]
    for s in IN_SPLITS:
        offs.append(offs[-1] + s)
    a_qkv, a_gate, a_beta, a_alpha, b_qkv, b_f, c_z, c_xbc, c_dt, gates = [
        w_in[:, offs[i]:offs[i + 1]] for i in range(len(IN_SPLITS))]
    small = jnp.concatenate([a_beta, a_alpha, b_f, c_dt], axis=1)
    small = jnp.pad(small, ((0, 0), (0, 128 - small.shape[1])))
    pad = jnp.zeros((w_in.shape[0], N_PROJ - C_SMALL - 128), w_in.dtype)
    out = jnp.concatenate([gates, a_qkv, a_gate, b_qkv, c_z, c_xbc, small, pad], axis=1)
    return out.astype(BF16)


def _ada_kernel(c_ref, w_ref, b_ref, o_ref):
    c = c_ref[...]
    a = (c * jax.nn.sigmoid(c)).astype(BF16)
    o_ref[...] = jnp.dot(a, w_ref[...], preferred_element_type=F32) + b_ref[...]


def _adaln(c, w, b):
    m, k = c.shape
    n = w.shape[1]
    tn = 512
    return pl.pallas_call(
        _ada_kernel,
        grid=(n // tn,),
        in_specs=[pl.BlockSpec((m, k), lambda j: (0, 0)),
                  pl.BlockSpec((k, tn), lambda j: (0, j)),
                  pl.BlockSpec((1, tn), lambda j: (0, j))],
        out_specs=pl.BlockSpec((m, tn), lambda j: (0, j)),
        out_shape=jax.ShapeDtypeStruct((m, n), F32),
        compiler_params=_cparams("parallel"),
        name="adaln",
    )(c, w, b.reshape(1, n))


def _mod_operand(m, seq_len, tm):
    bsz, d = m.shape
    if seq_len % tm == 0:
        per = seq_len // tm
        return m.reshape(bsz, 1, d), pl.BlockSpec((None, 1, d), lambda i, *_: (i // per, 0, 0))
    arr = jnp.repeat(m, seq_len, axis=0)
    return arr, pl.BlockSpec((tm, d), lambda i, *_: (i, 0))


def _rms_mod(x, nw, sc, sh):
    y = x * lax.rsqrt(jnp.mean(x * x, axis=-1, keepdims=True) + EPS)
    return (y * nw) * (1.0 + sc) + sh


def _proj_kernel(x_ref, nw_ref, sc_ref, sh_ref, w_ref, o_ref, h_scr):
    @pl.when(pl.program_id(1) == 0)
    def _():
        h_scr[...] = _rms_mod(x_ref[...], nw_ref[...], sc_ref[...], sh_ref[...]).astype(BF16)

    o_ref[...] = jnp.dot(h_scr[...], w_ref[...], preferred_element_type=F32)


def _proj_in(x, nw, sc, sh, w, seq_len):
    t, d = x.shape
    n = w.shape[1]
    tm = min(512, t)
    tn = 1024
    sc_a, sc_s = _mod_operand(sc, seq_len, tm)
    sh_a, sh_s = _mod_operand(sh, seq_len, tm)
    return pl.pallas_call(
        _proj_kernel,
        grid=(t // tm, n // tn),
        in_specs=[pl.BlockSpec((tm, d), lambda i, j: (i, 0)),
                  pl.BlockSpec((1, d), lambda i, j: (0, 0)),
                  sc_s, sh_s,
                  pl.BlockSpec((d, tn), lambda i, j: (0, j))],
        out_specs=pl.BlockSpec((tm, tn), lambda i, j: (i, j)),
        out_shape=jax.ShapeDtypeStruct((t, n), F32),
        scratch_shapes=[pltpu.VMEM((tm, d), BF16)],
        compiler_params=_cparams("parallel", "arbitrary"),
        name="proj_in",
    )(x, nw.reshape(1, d), sc_a, sh_a, w)


def _split(x, parts):
    out = []
    for _ in range(parts - 1):
        hi = x.astype(BF16)
        out.append(hi)
        x = x - hi.astype(F32)
    out.append(x.astype(BF16))
    return out


def _mxu(a, b, dims):
    return lax.dot_general(a, b, (dims, ((), ())), preferred_element_type=F32)


def _dot3(a, b, dims):
    ah, al = _split(a, 2)
    bh, bl = _split(b, 2)
    return _mxu(ah, bh, dims) + (_mxu(ah, bl, dims) + _mxu(al, bh, dims))


def _dot(a, b):
    return _dot3(a, b, ((1,), (0,)))


def _dot_nt(a, b):
    return _dot3(a, b, ((1,), (1,)))


def _mask_dot(mask, x):
    m = mask.astype(BF16)
    return sum(_mxu(m, p, ((1,), (0,))) for p in _split(x, 3))


def _dot_mask(x, mask):
    m = mask.astype(BF16)
    return sum(_mxu(p, m, ((1,), (0,))) for p in _split(x, 3))


def _cumsum_maps(g, c):
    row = lax.broadcasted_iota(jnp.int32, (c, c), 0)
    col = lax.broadcasted_iota(jnp.int32, (c, c), 1)
    tril = row >= col
    gb = jnp.broadcast_to(g, (c, c))
    ri = _mask_dot(tril, gb)
    rj = _mask_dot(jnp.ones((c, c), BF16), jnp.where(row <= col, gb, 0.0))
    return ri, rj, tril, row, col


def _gdn_head(q, k, v, g, beta, s, c, levels):
    ri, rj, tril, row, col = _cumsum_maps(g, c)
    decay = jnp.where(tril, jnp.exp(jnp.where(tril, ri - rj, 0.0)), 0.0)
    gc = ri[:, 0:1]
    gc_last = ri[c - 1:c, 0:1]
    kb = k * beta
    m = jnp.where(row > col, _dot_nt(kb, k) * decay, 0.0)
    eye = (row == col).astype(F32)
    blk = (row // 8) == (col // 8)
    p = jnp.where(blk, m, 0.0)
    x = eye - p
    for _ in range(levels[0]):
        p = _dot(p, p)
        x = x + _dot(x, p)
    b = 8
    for _ in range(levels[1]):
        low = jnp.where(((row // (2 * b)) == (col // (2 * b))) & ((row // b) != (col // b)), m, 0.0)
        x = x - _dot(_dot(x, low), x)
        b *= 2
    u_in = _dot(x, v * beta)
    wk_in = _dot(x, kb * jnp.exp(gc))
    qk = _dot_nt(q, k) * decay
    q_dec = q * jnp.exp(gc)
    k_dec = k * jnp.exp(gc_last - gc)
    u = u_in - _dot(wk_in, s)
    o = _dot(q_dec, s) + _dot(qk, u)
    s_new = s * jnp.exp(gc_last) + _dot(k_dec.T, u)
    return o, s_new


def _gdn_kernel(q_ref, k_ref, v_ref, gb_ref, gate_ref, nw_ref, s0_ref, o_ref, sout_ref, s_scr,
                *, c, levels):
    ci = pl.program_id(1)

    @pl.when(ci == 0)
    def _():
        s_scr[...] = s0_ref[...]

    for bi in range(s_scr.shape[0]):
        for hd in range(GDN_HEADS):
            sl = slice(hd * GDN_DV, (hd + 1) * GDN_DV)
            o, s_new = _gdn_head(q_ref[bi, :, sl], k_ref[bi, :, sl], v_ref[bi, :, sl],
                                 gb_ref[bi, :, hd:hd + 1],
                                 gb_ref[bi, :, GDN_HEADS + hd:GDN_HEADS + hd + 1], s_scr[bi, hd], c, levels)
            s_scr[bi, hd] = s_new
            sout_ref[bi, hd] = s_new
            y = o * lax.rsqrt(jnp.mean(o * o, axis=-1, keepdims=True) + EPS) * nw_ref[...]
            gt = gate_ref[bi, :, sl]
            o_ref[bi, :, sl] = (y * (gt * jax.nn.sigmoid(gt))).astype(o_ref.dtype)


def _batch_block(b, l, c):
    return 2 if b % 2 == 0 else 1


def _gdn_scan(qkv, gb, gate, gate_col, nw, s0, n_valid):
    b, l, _ = qkv.shape
    c = min(CHUNK, l)
    bb = _batch_block(b, l, c)
    n_eff = min(n_valid, c)
    base = min(n_eff, 8)
    levels = (max(0, math.ceil(math.log2(base)) - 1) if base > 1 else 0,
              max(0, math.ceil(math.log2(n_eff / 8))) if n_eff > 8 else 0)
    col = lambda j: pl.BlockSpec((bb, c, MIX_W), lambda bi, ci: (bi, ci, j))
    st = pl.BlockSpec((bb, GDN_HEADS, GDN_DK, GDN_DV), lambda bi, ci: (bi, 0, 0, 0))
    return pl.pallas_call(
        functools.partial(_gdn_kernel, c=c, levels=levels),
        grid=(b // bb, l // c),
        in_specs=[col(0), col(1), col(2),
                  pl.BlockSpec((bb, c, 8), lambda bi, ci: (bi, ci, 0)),
                  col(gate_col),
                  pl.BlockSpec((1, GDN_DV), lambda bi, ci: (0, 0)),
                  st],
        out_specs=[col(0), st],
        out_shape=[jax.ShapeDtypeStruct((b, l, MIX_W), BF16),
                   jax.ShapeDtypeStruct((b, GDN_HEADS, GDN_DK, GDN_DV), F32)],
        scratch_shapes=[pltpu.VMEM((bb, GDN_HEADS, GDN_DK, GDN_DV), F32)],
        compiler_params=_cparams("parallel", "arbitrary"),
        name="gdn_scan",
    )(qkv, qkv, qkv, gb, gate, nw.reshape(1, GDN_DV), s0)


def _ssd_kernel(x_ref, da_ref, z_ref, dskip_ref, nw_ref, h0_ref, y_ref, hout_ref, h_scr, *, c):
    ci = pl.program_id(1)

    @pl.when(ci == 0)
    def _():
        h_scr[...] = h0_ref[...]

    lane = lax.broadcasted_iota(jnp.int32, (c, 128), 1)
    lane_lo = lane < 64
    hrow = lax.broadcasted_iota(jnp.int32, (128, 128), 0) < 64
    for bi in range(h_scr.shape[0]):
        ys = []
        for pi in range(4):
            gmask = (lane_lo if pi < 2 else jnp.logical_not(lane_lo)).astype(F32)
            x = x_ref[bi, :, pi * 128:(pi + 1) * 128]
            bm = x_ref[bi, :, MIX_W:MIX_W + 128] * gmask
            cm = x_ref[bi, :, MIX_W + 128:MIX_W + 256] * gmask
            hst = h_scr[bi, pi]
            cb = _dot_nt(cm, bm)
            y = jnp.zeros((c, 128), F32)
            h_new = jnp.zeros((128, 128), F32)
            for hh in range(2):
                hd = 2 * pi + hh
                dt = da_ref[bi, :, hd:hd + 1]
                da = da_ref[bi, :, 8 + hd:9 + hd]
                ri, rj, tril, _, _ = _cumsum_maps(da, c)
                seg = jnp.where(tril, jnp.exp(jnp.where(tril, ri - rj, 0.0)), 0.0)
                a = ri[:, 0:1]
                a_last = ri[c - 1:c, 0:1]
                hm = lane_lo if hh == 0 else jnp.logical_not(lane_lo)
                xdt = jnp.where(hm, x, 0.0) * dt
                y_h = _dot(cb * seg, xdt) + _dot_nt(cm * jnp.exp(a), hst)
                y = y + jnp.where(hm, y_h, 0.0)
                upd = _dot(xdt.T, bm * jnp.exp(a_last - a))
                hsel = hrow if hh == 0 else jnp.logical_not(hrow)
                h_new = h_new + jnp.where(hsel, hst * jnp.exp(a_last) + upd, 0.0)
            h_scr[bi, pi] = h_new
            hout_ref[bi, pi] = h_new
            ys.append(y)
        y = jnp.concatenate(ys, axis=1) + dskip_ref[...] * x_ref[bi, :, 0:MIX_W]
        z = z_ref[bi]
        yz = y * (z * jax.nn.sigmoid(z))
        y_ref[bi] = (yz * lax.rsqrt(jnp.mean(yz * yz, axis=-1, keepdims=True) + EPS)
                     * nw_ref[...]).astype(y_ref.dtype)


def _ssd_scan(xbc, da, z, z_col, dskip, nw, h0):
    b, l, w = xbc.shape
    c = min(CHUNK, l)
    bb = _batch_block(b, l, c)
    st = pl.BlockSpec((bb, 4, 128, 128), lambda bi, ci: (bi, 0, 0, 0))
    vec = pl.BlockSpec((1, MIX_W), lambda bi, ci: (0, 0))
    return pl.pallas_call(
        functools.partial(_ssd_kernel, c=c),
        grid=(b // bb, l // c),
        in_specs=[pl.BlockSpec((bb, c, w), lambda bi, ci: (bi, ci, 0)),
                  pl.BlockSpec((bb, c, 16), lambda bi, ci: (bi, ci, 0)),
                  pl.BlockSpec((bb, c, MIX_W), lambda bi, ci: (bi, ci, z_col)),
                  vec, vec, st],
        out_specs=[pl.BlockSpec((bb, c, MIX_W), lambda bi, ci: (bi, ci, 0)), st],
        out_shape=[jax.ShapeDtypeStruct((b, l, MIX_W), BF16),
                   jax.ShapeDtypeStruct((b, 4, 128, 128), F32)],
        scratch_shapes=[pltpu.VMEM((bb, 4, 128, 128), F32)],
        compiler_params=_cparams("parallel", "arbitrary"),
        name="ssd_scan",
    )(xbc, da, z, dskip.reshape(1, MIX_W), nw.reshape(1, MIX_W), h0)


def _fox_prompt_kernel(q_ref, k_ref, v_ref, cq_ref, ck_ref, o_ref, kb_scr, vb_scr,
                       m_scr, l_scr, acc_scr, qh_scr, cq_scr, *, tq, tk):
    i = pl.program_id(2)

    @pl.when(i == 0)
    def _():
        kb_scr[...] = k_ref[...].astype(BF16)
        vb_scr[...] = v_ref[...].astype(BF16)

    lane = lax.broadcasted_iota(jnp.int32, (tq, 128), 1)
    q = q_ref[...] * (FOX_SCALE * LOG2E)
    rep = tk // 128
    for hh in range(2):
        hm = (lane < 64) if hh == 0 else (lane >= 64)
        qh_scr[hh] = jnp.where(hm, q, 0.0).astype(BF16)
        cq_scr[hh] = jnp.concatenate([jnp.broadcast_to(cq_ref[:, hh:hh + 1], (tq, 128))] * rep, axis=1)
        m_scr[hh] = jnp.full((tq, 128), -jnp.inf, F32)
        l_scr[hh] = jnp.zeros((tq, 128), F32)
        acc_scr[hh] = jnp.zeros((tq, 128), F32)

    def step(j, masked):
        off = pl.multiple_of(j * tk, tk)
        kk = kb_scr[pl.ds(off, tk), :]
        vv = vb_scr[pl.ds(off, tk), :]
        for hh in range(2):
            s = lax.dot_general(qh_scr[hh], kk, (((1,), (1,)), ((), ())),
                                preferred_element_type=F32)
            s = s + cq_scr[hh] - ck_ref[j, hh:hh + 1, :]
            if masked:
                rowp = lax.broadcasted_iota(jnp.int32, (tq, tk), 0)
                colp = lax.broadcasted_iota(jnp.int32, (tq, tk), 1)
                s = jnp.where(colp <= rowp, s, -jnp.inf)
            m_prev = m_scr[hh]
            m_new = jnp.maximum(m_prev, jnp.max(s, axis=1, keepdims=True))
            alpha = jnp.exp2(m_prev - m_new)
            p = jnp.exp2(s - jnp.concatenate([m_new] * rep, axis=1))
            l_scr[hh] = alpha * l_scr[hh] + jnp.sum(p, axis=1, keepdims=True)
            acc_scr[hh] = alpha * acc_scr[hh] + jnp.dot(p.astype(BF16), vv,
                                                        preferred_element_type=F32)
            m_scr[hh] = m_new

    def body(j, carry):
        step(j, False)
        return carry

    lax.fori_loop(0, i, body, 0)
    step(i, True)
    o_ref[...] = jnp.where(lane < 64, acc_scr[0] / l_scr[0], acc_scr[1] / l_scr[1]).astype(o_ref.dtype)


def _fox_prompt(proj, cum, bsz, seq):
    tq = tk = 512 if seq % 512 == 0 else seq
    nq = seq // tq
    cum = cum * LOG2E
    cq = cum.reshape(bsz, seq, 4, 2).transpose(0, 2, 1, 3)
    ck = cum.reshape(bsz, nq, tk, 4, 2).transpose(0, 3, 1, 4, 2)
    qb = C_BQKV // 128
    kb = (C_BQKV + 512) // 128
    vb = (C_BQKV + 1024) // 128
    return pl.pallas_call(
        functools.partial(_fox_prompt_kernel, tq=tq, tk=tk),
        grid=(bsz, 4, nq),
        in_specs=[pl.BlockSpec((tq, 128), lambda b, p, i: (b * nq + i, qb + p)),
                  pl.BlockSpec((seq, 128), lambda b, p, i: (b, kb + p)),
                  pl.BlockSpec((seq, 128), lambda b, p, i: (b, vb + p)),
                  pl.BlockSpec((None, None, tq, 2), lambda b, p, i: (b, p, i, 0)),
                  pl.BlockSpec((None, None, nq, 2, tk), lambda b, p, i: (b, p, 0, 0, 0))],
        out_specs=pl.BlockSpec((tq, 128), lambda b, p, i: (b * nq + i, p)),
        out_shape=jax.ShapeDtypeStruct((bsz * seq, 512), BF16),
        scratch_shapes=[pltpu.VMEM((seq, 128), BF16), pltpu.VMEM((seq, 128), BF16),
                        pltpu.VMEM((2, tq, 128), F32), pltpu.VMEM((2, tq, 128), F32),
                        pltpu.VMEM((2, tq, 128), F32), pltpu.VMEM((2, tq, 128), BF16),
                        pltpu.VMEM((2, tq, tk), F32)],
        compiler_params=_cparams("parallel", "parallel", "arbitrary"),
        name="fox_prompt",
    )(proj, proj, proj, cq, ck)


def _fox_sample_kernel(pt_ref, q_ref, kn_ref, vn_ref, ln_ref, *rest, n_new, pps):
    page_refs = [rest[3 * i:3 * i + 3] for i in range(pps)]
    o_ref, m_scr, l_scr, acc_scr, cs_scr, suf_scr = rest[3 * pps:]
    s_id = pl.program_id(1)
    n_steps = pl.num_programs(1)
    r = 8 * n_new
    rowi = lax.broadcasted_iota(jnp.int32, (r, MIX_W), 0)
    lanei = lax.broadcasted_iota(jnp.int32, (r, MIX_W), 1)
    hmask = (lanei // FOX_DH) == (rowi % 8)
    q = q_ref[...] * FOX_SCALE
    qrep = jnp.concatenate([jnp.broadcast_to(q[t:t + 1, :], (8, MIX_W)) for t in range(n_new)], axis=0)
    qbd = jnp.where(hmask, qrep, 0.0).astype(BF16)
    krow = lax.broadcasted_iota(jnp.int32, (128, 128), 0)
    kcol = lax.broadcasted_iota(jnp.int32, (128, 128), 1)
    rr = lax.broadcasted_iota(jnp.int32, (r, 128), 0)
    kk = lax.broadcasted_iota(jnp.int32, (r, 128), 1)

    def attend(parts):
        ss = []
        for kx, _, bias, valid in parts:
            s = jnp.dot(qbd, kx.astype(BF16), preferred_element_type=F32) + bias
            ss.append(s if valid is None else jnp.where(valid, s, -jnp.inf))
        m_new = m_prev = m_scr[...]
        for s in ss:
            m_new = jnp.maximum(m_new, jnp.max(s, axis=1, keepdims=True))
        alpha = jnp.exp(m_prev - m_new)
        l_new = alpha * l_scr[...]
        acc = jnp.concatenate([alpha] * (MIX_W // 128), axis=1) * acc_scr[...]
        for (_, vx, _, _), s in zip(parts, ss):
            p = jnp.exp(s - m_new)
            l_new = l_new + jnp.sum(p, axis=1, keepdims=True)
            acc = acc + lax.dot_general(p.astype(BF16), vx.astype(BF16), (((1,), (1,)), ((), ())),
                                        preferred_element_type=F32)
        l_scr[...] = l_new
        acc_scr[...] = acc
        m_scr[...] = m_new

    @pl.when(s_id == 0)
    def _():
        m_scr[...] = jnp.full((r, 128), -jnp.inf, F32)
        l_scr[...] = jnp.zeros((r, 128), F32)
        acc_scr[...] = jnp.zeros((r, MIX_W), F32)
        suf_scr[...] = jnp.zeros((r, 128), F32)
        lf = jnp.concatenate([ln_ref[...]] * n_new, axis=0)
        cum = _dot_mask(lf, krow <= kcol)
        tq = rr // 8
        cs = jnp.sum(jnp.where(kk == tq, cum, 0.0), axis=1, keepdims=True)
        cs_scr[...] = jnp.broadcast_to(cs, (r, 128))
        attend([(kn_ref[...], vn_ref[...], cs - cum, kk <= tq)])

    @pl.when(s_id > 0)
    def _():
        suf = suf_scr[...]
        cs = cs_scr[...]
        parts = []
        for kp_ref, vp_ref, lp_ref in page_refs:
            lf = jnp.concatenate([lp_ref[...]] * n_new, axis=0)
            after = _dot_mask(lf, krow > kcol)
            parts.append((kp_ref[...].reshape(MIX_W, PAGE_SIZE), vp_ref[...].reshape(MIX_W, PAGE_SIZE),
                          cs + suf + after, None))
            suf = suf + jnp.sum(lf, axis=1, keepdims=True)
        attend(parts)
        suf_scr[...] = suf

    @pl.when(s_id == n_steps - 1)
    def _():
        o = jnp.where(hmask, acc_scr[...] / jnp.concatenate([l_scr[...]] * (MIX_W // 128), axis=1), 0.0)
        o_ref[...] = jnp.sum(o.reshape(n_new, 8, MIX_W), axis=1)


def _fox_sample(q, k_new, v_new, logf_new, cache_k, cache_v, cache_lf, page_table, layer):
    bd, n_new, _ = q.shape
    n_pages = page_table.shape[1]
    r = 8 * n_new
    key_last = lambda a: _pad_seq(a, 1, PAGE_SIZE).transpose(0, 2, 1)
    pps = 2 if n_pages % 2 == 0 else 1

    def page(i):
        return lambda b, s, pt: pt[b, n_pages - 1 - ((jnp.maximum(s, 1) - 1) * pps + i)]

    per_b = lambda shape: pl.BlockSpec((None,) + shape, lambda b, s, pt: (b, 0, 0))
    page_specs, page_args = [], []
    for i in range(pps):
        pg = page(i)
        kv = pl.BlockSpec((None, None, FOX_HEADS, FOX_DH, PAGE_SIZE),
                          lambda b, s, pt, pg=pg: (layer, pg(b, s, pt), 0, 0, 0))
        lf = pl.BlockSpec((None, None, FOX_HEADS, PAGE_SIZE),
                          lambda b, s, pt, pg=pg: (layer, pg(b, s, pt), 0, 0))
        page_specs += [kv, kv, lf]
        page_args += [cache_k, cache_v, cache_lf]
    grid_spec = pltpu.PrefetchScalarGridSpec(
        num_scalar_prefetch=1,
        grid=(bd, n_pages // pps + 1),
        in_specs=[per_b((n_new, MIX_W)), per_b((MIX_W, PAGE_SIZE)), per_b((MIX_W, PAGE_SIZE)),
                  per_b((FOX_HEADS, PAGE_SIZE))] + page_specs,
        out_specs=per_b((n_new, MIX_W)),
        scratch_shapes=[pltpu.VMEM((r, 128), F32), pltpu.VMEM((r, 128), F32),
                        pltpu.VMEM((r, MIX_W), F32), pltpu.VMEM((r, 128), F32),
                        pltpu.VMEM((r, 128), F32)])
    o = pl.pallas_call(
        functools.partial(_fox_sample_kernel, n_new=n_new, pps=pps),
        grid_spec=grid_spec,
        out_shape=jax.ShapeDtypeStruct((bd, n_new, MIX_W), F32),
        compiler_params=_cparams("parallel", "arbitrary"),
        name="fox_sample",
    )(page_table, q, key_last(k_new), key_last(v_new), key_last(logf_new), *page_args)
    return o.reshape(bd * n_new, MIX_W)


def _merge_kernel(oa_ref, ob_ref, oc_ref, g0_ref, g1_ref, g2_ref, wb_ref, wo_ref, x_ref,
                  gate_ref, nw_ref, sc_ref, sh_ref, xo_ref, h_ref):
    acc = None
    for br, (o_r, g_r) in enumerate(((oa_ref, g0_ref), (ob_ref, g1_ref), (oc_ref, g2_ref))):
        y = jnp.dot(o_r[...].astype(BF16), wb_ref[br], preferred_element_type=F32)
        y = jax.nn.sigmoid(g_r[...]) * y
        acc = y if acc is None else acc + y
    out = jnp.dot(acc.astype(BF16), wo_ref[...], preferred_element_type=F32)
    xn = x_ref[...] + gate_ref[...] * out
    xo_ref[...] = xn
    h_ref[...] = _rms_mod(xn, nw_ref[...], sc_ref[...], sh_ref[...])


def _merge(o_a, o_b, o_c, proj, wb, wo, x, g1, nw, sc2, sh2, seq_len):
    t, d = x.shape
    tm = min(256, t)
    g_a, g_s = _mod_operand(g1, seq_len, tm)
    sc_a, sc_s = _mod_operand(sc2, seq_len, tm)
    sh_a, sh_s = _mod_operand(sh2, seq_len, tm)
    br = pl.BlockSpec((tm, MIX_W), lambda i: (i, 0))
    row = pl.BlockSpec((tm, d), lambda i: (i, 0))
    return pl.pallas_call(
        _merge_kernel,
        grid=(t // tm,),
        in_specs=[br, br, br,
                  pl.BlockSpec((tm, d), lambda i: (i, 0)),
                  pl.BlockSpec((tm, d), lambda i: (i, 1)),
                  pl.BlockSpec((tm, d), lambda i: (i, 2)),
                  pl.BlockSpec((3, MIX_W, d), lambda i: (0, 0, 0)),
                  pl.BlockSpec((d, d), lambda i: (0, 0)),
                  row, g_s,
                  pl.BlockSpec((1, d), lambda i: (0, 0)),
                  sc_s, sh_s],
        out_specs=[row, row],
        out_shape=[jax.ShapeDtypeStruct((t, d), F32), jax.ShapeDtypeStruct((t, d), F32)],
        compiler_params=_cparams("parallel"),
        name="merge",
    )(o_a, o_b, o_c, proj, proj, proj, wb, wo, x, g_a, nw.reshape(1, d), sc_a, sh_a)


def _top16_rows(s):
    rows, tb = s.shape
    rid = lax.broadcasted_iota(jnp.int32, (rows, tb), 0).astype(F32)
    out_row = lax.broadcasted_iota(jnp.int32, (PEER_TOPK, tb), 0)
    vals = jnp.zeros((PEER_TOPK, tb), F32)
    idxs = jnp.zeros((PEER_TOPK, tb), F32)
    for it in range(PEER_TOPK):
        mx = jnp.max(s, axis=0, keepdims=True)
        ix = jnp.min(jnp.where(s == mx, rid, float(rows)), axis=0, keepdims=True)
        vals = jnp.where(out_row == it, mx, vals)
        idxs = jnp.where(out_row == it, ix, idxs)
        s = jnp.where(rid == ix, -jnp.inf, s)
    return vals, idxs


_CAND_GROUPS = (("i", 0, 0, 16), ("j", 0, 1, 16), ("i", 1, 1, 8), ("j", 1, 2, 8),
                ("i", 2, 2, 5), ("i", 3, 2, 4), ("i", 4, 2, 3))


def _candidates(a, b, ia, ib):
    tb = a.shape[1]
    vals, poss, cids = [], [], []
    for kind, fixed, lo, hi in _CAND_GROUPS:
        n = 16 if hi > 8 else 8
        r = lax.broadcasted_iota(jnp.int32, (n, tb), 0)
        ok = (r >= lo) & (r < hi)
        if kind == "i":
            v = a[fixed:fixed + 1] + b[0:n]
            c = ia[fixed:fixed + 1] * float(PEER_NKEYS) + ib[0:n]
            p = fixed * PEER_TOPK + r
        else:
            v = a[0:n] + b[fixed:fixed + 1]
            c = ia[0:n] * float(PEER_NKEYS) + ib[fixed:fixed + 1]
            p = r * PEER_TOPK + fixed
        vals.append(jnp.where(ok, v, -jnp.inf))
        poss.append(jnp.where(ok, p, 4096).astype(F32))
        cids.append(c)
    return (jnp.concatenate(vals, axis=0), jnp.concatenate(poss, axis=0),
            jnp.concatenate(cids, axis=0))


def _route_kernel(h_ref, wq_ref, keys_ref, eid_ref, gate_ref, *, tb):
    qt = lax.dot_general(wq_ref[...], h_ref[...].astype(BF16), (((1,), (1,)), ((), ())),
                         preferred_element_type=F32)
    out_row = lax.broadcasted_iota(jnp.int32, (PEER_TOPK, tb), 0)
    tops, eids = [], []
    for hd in range(PEER_HEADS):
        sv, si = [], []
        for half in range(2):
            c0 = (hd * 2 + half) * 128
            sc = jnp.dot(keys_ref[half], qt[c0:c0 + 128, :].astype(BF16),
                         preferred_element_type=F32)
            v_, i_ = _top16_rows(sc)
            sv.append(v_)
            si.append(i_)
        cand, cpos, cid = _candidates(sv[0], sv[1], si[0], si[1])
        top = jnp.zeros((PEER_TOPK, tb), F32)
        eid = jnp.zeros((PEER_TOPK, tb), F32)
        for it in range(PEER_TOPK):
            mx = jnp.max(cand, axis=0, keepdims=True)
            px = jnp.min(jnp.where(cand == mx, cpos, 8192.0), axis=0, keepdims=True)
            hit = cpos == px
            ev = jnp.sum(jnp.where(hit, cid, 0.0), axis=0, keepdims=True)
            top = jnp.where(out_row == it, mx, top)
            eid = jnp.where(out_row == it, ev, eid)
            cand = jnp.where(hit, -jnp.inf, cand)
        e = jnp.exp(top - top[0:1])
        tops.append(e / jnp.sum(e, axis=0, keepdims=True))
        eids.append(eid)
    gate_ref[...] = jnp.concatenate(tops, axis=0)
    eid_ref[...] = jnp.concatenate(eids, axis=0).T.astype(jnp.int32)


def _route(h, wq_t, keys):
    t, d = h.shape
    tb = 128
    return pl.pallas_call(
        functools.partial(_route_kernel, tb=tb),
        grid=(t // tb,),
        in_specs=[pl.BlockSpec((tb, d), lambda i: (i, 0)),
                  pl.BlockSpec((PEER_HEADS * PEER_DQ, d), lambda i: (0, 0)),
                  pl.BlockSpec((2, 128, 128), lambda i: (0, 0, 0))],
        out_specs=[pl.BlockSpec((tb, 128), lambda i: (i, 0)),
                   pl.BlockSpec((None, 128, tb), lambda i: (i, 0, 0))],
        out_shape=[jax.ShapeDtypeStruct((t, 128), jnp.int32),
                   jax.ShapeDtypeStruct((t // tb, 128, tb), F32)],
        compiler_params=_cparams("parallel"),
        name="peer_route",
    )(h, wq_t, keys)


GATHER_WINDOW = 64
PEER_CHUNKS = 4


def _sc_gather(table, idx):
    n = idx.shape[0]
    width = table.shape[1]
    mesh = plsc.VectorSubcoreMesh(core_axis_name="core", subcore_axis_name="subcore")
    n_workers = mesh.num_cores * mesh.num_subcores
    win = GATHER_WINDOW
    per_w = n // n_workers
    n_chunks = per_w // win
    assert per_w * n_workers == n and n_chunks * win == per_w and n_chunks % 2 == 0

    @pl.kernel(out_type=jax.ShapeDtypeStruct((n, width), table.dtype), mesh=mesh,
               scratch_types=[pltpu.VMEM((win,), jnp.int32), pltpu.VMEM((win,), jnp.int32),
                              pltpu.VMEM((win, width), table.dtype),
                              pltpu.VMEM((win, width), table.dtype),
                              pltpu.SemaphoreType.DMA, pltpu.SemaphoreType.DMA,
                              pltpu.SemaphoreType.DMA, pltpu.SemaphoreType.DMA],
               name="peer_gather")
    def gather_kernel(tab_hbm, idx_hbm, out_hbm, i0, i1, r0, r1, g0, g1, w0, w1):
        wid = lax.axis_index("subcore") * mesh.num_cores + lax.axis_index("core")
        base = wid * per_w
        bufs = ((i0, r0, g0, w0), (i1, r1, g1, w1))

        def write_copy(rv, ws, off):
            return pltpu.make_async_copy(rv, out_hbm.at[pl.ds(off, win)], ws)

        @pl.loop(0, n_chunks, step=2)
        def _(c):
            for b, (iv, rv, gs, ws) in enumerate(bufs):
                off = base + (c + b) * win

                @pl.when(c > 0)
                def _():
                    write_copy(rv, ws, off).wait()

                pltpu.sync_copy(idx_hbm.at[pl.ds(off, win)], iv)
                pltpu.make_async_copy(tab_hbm.at[iv], rv, gs).start()
            for b, (iv, rv, gs, ws) in enumerate(bufs):
                off = base + (c + b) * win
                pltpu.make_async_copy(tab_hbm.at[iv], rv, gs).wait()
                write_copy(rv, ws, off).start()

        for iv, rv, gs, ws in bufs:
            write_copy(rv, ws, base).wait()

    return gather_kernel(table, idx)


def _pack_table(tab):
    b = lax.bitcast_convert_type(tab.astype(BF16), jnp.uint16).astype(jnp.uint32)
    return lax.bitcast_convert_type(b[:, :512] | (b[:, 512:] << 16), jnp.int32)


def _unpack(w):
    lo = pltpu.bitcast(w << 16, F32)
    hi = pltpu.bitcast(w & jnp.int32(-65536), F32)
    return lo, hi


def _peer_kernel(gu_ref, gv_ref, h_ref, gt_ref, x_ref, g2_ref, o_ref, *, tb):
    base = (pl.program_id(0) * tb) % 128
    gt = gt_ref[...]
    tok = lax.broadcasted_iota(jnp.int32, (128, 128), 1)
    for t in range(tb):
        h_lo = h_ref[t:t + 1, 0:512]
        h_hi = h_ref[t:t + 1, 512:1024]
        u_lo, u_hi = _unpack(gu_ref[t * 128:(t + 1) * 128, :])
        pu = u_lo * h_lo + u_hi * h_hi
        p128 = pu[:, 0:128] + pu[:, 128:256] + pu[:, 256:384] + pu[:, 384:512]
        pre = jnp.sum(p128, axis=1, keepdims=True)
        act = 0.5 * pre * (1.0 + lax.erf(pre * (2.0 ** -0.5)))
        gcol = jnp.sum(jnp.where(tok == base + t, gt, 0.0), axis=1, keepdims=True)
        w = gcol * act
        v_lo, v_hi = _unpack(gv_ref[t * 128:(t + 1) * 128, :])
        y_lo = jnp.sum(v_lo * w, axis=0, keepdims=True)
        y_hi = jnp.sum(v_hi * w, axis=0, keepdims=True)
        o_ref[t:t + 1, 0:512] = x_ref[t:t + 1, 0:512] + g2_ref[t:t + 1, 0:512] * y_lo
        o_ref[t:t + 1, 512:1024] = x_ref[t:t + 1, 512:1024] + g2_ref[t:t + 1, 512:1024] * y_hi


def _peer_mix(gu, gv, h, gate_t, x, g2, seq_len, tok0=0):
    d = x.shape[1]
    n = gu.shape[0] // 128
    tb = 16
    assert tok0 % 128 == 0
    b0 = tok0 // tb
    if seq_len % tb == 0:
        per = seq_len // tb
        g_a = jnp.broadcast_to(g2.reshape(g2.shape[0], 1, d), (g2.shape[0], tb, d))
        g_s = pl.BlockSpec((None, tb, d), lambda i: ((i + b0) // per, 0, 0))
    else:
        g_a = jnp.repeat(g2, seq_len, axis=0)
        g_s = pl.BlockSpec((tb, d), lambda i: (i + b0, 0))
    row = pl.BlockSpec((tb, d), lambda i: (i + b0, 0))
    rows = pl.BlockSpec((tb * 128, 512), lambda i: (i, 0))
    per_grp = 128 // tb
    return pl.pallas_call(
        functools.partial(_peer_kernel, tb=tb),
        grid=(n // tb,),
        in_specs=[rows, rows, row,
                  pl.BlockSpec((None, 128, 128), lambda i: ((i + b0) // per_grp, 0, 0)),
                  row, g_s],
        out_specs=pl.BlockSpec((tb, d), lambda i: (i, 0)),
        out_shape=jax.ShapeDtypeStruct((n, d), F32),
        compiler_params=_cparams("parallel"),
        name="peer_mix",
    )(gu, gv, h, gate_t, x, g_a)


def _final_norm_kernel(x_ref, w_ref, o_ref):
    x = x_ref[...]
    o_ref[...] = x * lax.rsqrt(jnp.mean(x * x, axis=-1, keepdims=True) + EPS) * w_ref[...]


def _final_norm(x, w):
    t, d = x.shape
    tm = min(512, t)
    return pl.pallas_call(
        _final_norm_kernel,
        grid=(t // tm,),
        in_specs=[pl.BlockSpec((tm, d), lambda i: (i, 0)), pl.BlockSpec((1, d), lambda i: (0, 0))],
        out_specs=pl.BlockSpec((tm, d), lambda i: (i, 0)),
        out_shape=jax.ShapeDtypeStruct((t, d), F32),
        compiler_params=_cparams("parallel"),
        name="final_norm",
    )(x, w.reshape(1, d))


def _causal_conv(x, buf, w):
    l = x.shape[1]
    xp = jnp.concatenate([buf.astype(x.dtype), x], axis=1)
    y = sum(xp[:, i:i + l] * w[i] for i in range(CONV_W))
    return y, xp[:, l:]


def _l2norm(x):
    return x * lax.rsqrt(jnp.sum(x * x, axis=-1, keepdims=True) + EPS)


def _pad_seq(a, axis, lp):
    pad = lp - a.shape[axis]
    if pad == 0:
        return a
    cfg = [(0, 0)] * a.ndim
    cfg[axis] = (0, pad)
    return jnp.pad(a, cfg)


def _layer(x, mod, lw, states, bsz, seq, fox):
    sh1, sc1, g1, sh2, sc2, g2 = mod
    gdn_s, gdn_b, ssm_s, ssm_b = states
    t = bsz * seq
    chunk = CHUNK if seq >= CHUNK else SHORT_CHUNK
    lp = -(-seq // chunk) * chunk
    proj = _proj_in(x, lw["norm_mix"], sc1, sh1, lw["w_in"], seq)
    p3 = proj.reshape(bsz, seq, N_PROJ)
    small = p3[:, :, C_SMALL:C_SMALL + 24]
    a_beta, a_alpha, b_f, c_dt = small[..., 0:4], small[..., 4:8], small[..., 8:16], small[..., 16:24]

    a_conv, gdn_b_new = _causal_conv(p3[:, :, C_AQKV:C_AQKV + GDN_CONV_CH], gdn_b, lw["gdn_conv_w"])
    a_act = jax.nn.silu(a_conv)
    q = _l2norm(a_act[..., 0:512].reshape(bsz, seq, GDN_HEADS, GDN_DK)) * GDN_DK ** -0.5
    k = _l2norm(a_act[..., 512:1024].reshape(bsz, seq, GDN_HEADS, GDN_DK))
    qkv = jnp.concatenate([q.reshape(bsz, seq, MIX_W), k.reshape(bsz, seq, MIX_W),
                           a_act[..., 1024:1536]], axis=-1)
    beta = jax.nn.sigmoid(a_beta)
    g = -jnp.exp(lw["gdn_a_log"]) * jax.nn.softplus(a_alpha + lw["gdn_dt_bias"])
    gb = jnp.concatenate([g, beta], axis=-1)
    if seq % CHUNK == 0:
        gate_arr, gate_col = p3, C_AGATE // MIX_W
    else:
        gate_arr, gate_col = _pad_seq(p3[:, :, C_AGATE:C_AGATE + MIX_W], 1, lp), 0
    o_a, gdn_s_new = _gdn_scan(_pad_seq(qkv, 1, lp), _pad_seq(gb, 1, lp), gate_arr, gate_col,
                               lw["gdn_norm_w"], gdn_s, seq)
    o_a = o_a[:, :seq].reshape(t, MIX_W)

    logf = jax.nn.log_sigmoid(b_f + lw["fox_f_bias"])
    kf = p3[:, :, C_BQKV + 512:C_BQKV + 1024].reshape(bsz, seq, FOX_HEADS, FOX_DH)
    vf = p3[:, :, C_BQKV + 1024:C_BQKV + 1536].reshape(bsz, seq, FOX_HEADS, FOX_DH)
    o_b = fox(proj, p3, logf)

    xbc, ssm_b_new = _causal_conv(p3[:, :, C_CXBC:C_CXBC + SSM_CONV_CH], ssm_b, lw["ssm_conv_w"])
    xbc = jax.nn.silu(xbc + lw["ssm_conv_b"])
    dt = jax.nn.softplus(c_dt + lw["ssm_dt_bias"])
    da = jnp.concatenate([dt, dt * (-jnp.exp(lw["ssm_a_log"]))], axis=-1)
    if seq % CHUNK == 0:
        z_arr, z_col = p3, C_CZ // MIX_W
    else:
        z_arr, z_col = _pad_seq(p3[:, :, C_CZ:C_CZ + MIX_W], 1, lp), 0
    o_c, ssm_s_new = _ssd_scan(_pad_seq(xbc, 1, lp), _pad_seq(da, 1, lp), z_arr, z_col,
                               jnp.repeat(lw["ssm_d"], SSM_P), lw["ssm_norm_w"], ssm_s)
    o_c = o_c[:, :seq].reshape(t, MIX_W)

    x, h2 = _merge(o_a, o_b, o_c, proj, lw["w_branch"], lw["w_out"], x, g1, lw["norm_ffn"],
                   sc2, sh2, seq)
    eid, gate_t = _route(h2, lw["peer_wq_t"], lw["peer_keys"])
    n_chunks = PEER_CHUNKS if t % (PEER_CHUNKS * 1024) == 0 else 1
    tc = t // n_chunks
    parts = []
    for c in range(n_chunks):
        flat = eid[c * tc:(c + 1) * tc].reshape(-1)
        gu = _sc_gather(lw["peer_u"], flat)
        gv = _sc_gather(lw["peer_v"], flat)
        parts.append(_peer_mix(gu, gv, h2, gate_t, x, g2, seq, c * tc))
    x = parts[0] if n_chunks == 1 else jnp.concatenate(parts, axis=0)
    return x, (kf, vf, logf, gdn_s_new, gdn_b_new, ssm_s_new, ssm_b_new)


def _ssm_state_pack(s):
    b = s.shape[0]
    sp = s.reshape(b, 4, 128, SSM_N)
    z = jnp.zeros_like(sp)
    lo = jnp.concatenate([sp, z], axis=-1)
    hi = jnp.concatenate([z, sp], axis=-1)
    grp = (jnp.arange(4) // 2).reshape(1, 4, 1, 1)
    return jnp.where(grp == 0, lo, hi)


def _ssm_state_unpack(sp):
    b = sp.shape[0]
    out = jnp.where((jnp.arange(4) // 2).reshape(1, 4, 1, 1) == 0, sp[..., :SSM_N], sp[..., SSM_N:])
    return out.reshape(b, SSM_HEADS, SSM_P, SSM_N)


def kernel(x_prompt, x_sample, cache_k, cache_v, cache_logf, page_table, state_gdn, state_gdn_conv,
           state_ssm, state_ssm_conv, c_prompt, c_sample, w_ada, b_ada, norm_mix, norm_ffn, w_in,
           gdn_conv_w, gdn_a_log, gdn_dt_bias, gdn_norm_w, fox_f_bias, ssm_conv_w, ssm_conv_b,
           ssm_a_log, ssm_dt_bias, ssm_d, ssm_norm_w, w_branch, w_out, peer_wq, peer_keys, peer_u,
           peer_v, final_norm):
    n_pr, seq, d = x_prompt.shape
    n_dec, dec_seq, _ = x_sample.shape
    n_pool = cache_k.shape[1]
    xp = x_prompt.reshape(n_pr * seq, d)
    xs = x_sample.reshape(n_dec * dec_seq, d)
    c_all = jnp.concatenate([c_prompt, c_sample], axis=0)
    n_c = c_all.shape[0]
    c_all = jnp.pad(c_all, ((0, -n_c % 8), (0, 0)))
    cache_kt = cache_k.transpose(0, 1, 3, 4, 2)
    cache_vt = cache_v.transpose(0, 1, 3, 4, 2)
    cache_lf = cache_logf.transpose(0, 1, 3, 2)
    new_p, new_s = [], []
    for l in range(DEPTH):
        lw = {"norm_mix": norm_mix[l], "norm_ffn": norm_ffn[l], "w_in": _pack_w_in(w_in[l]),
              "gdn_conv_w": gdn_conv_w[l], "gdn_a_log": gdn_a_log[l], "gdn_dt_bias": gdn_dt_bias[l],
              "gdn_norm_w": gdn_norm_w[l], "fox_f_bias": fox_f_bias[l], "ssm_conv_w": ssm_conv_w[l],
              "ssm_conv_b": ssm_conv_b[l], "ssm_a_log": ssm_a_log[l], "ssm_dt_bias": ssm_dt_bias[l],
              "ssm_d": ssm_d[l], "ssm_norm_w": ssm_norm_w[l], "w_branch": w_branch[l].astype(BF16),
              "w_out": w_out[l].astype(BF16), "peer_wq_t": peer_wq[l].T.astype(BF16),
              "peer_keys": peer_keys[l].astype(BF16),
              "peer_u": _pack_table(peer_u[l]), "peer_v": _pack_table(peer_v[l])}
        mod = _adaln(c_all, w_ada[l].astype(BF16), b_ada[l])
        mod_p = jnp.split(mod[:n_pr], 6, axis=-1)
        mod_s = jnp.split(mod[n_pr:n_c], 6, axis=-1)

        def fox_p(proj, p3, logf):
            return _fox_prompt(proj, jnp.cumsum(logf, axis=1), n_pr, seq)

        zero_states = (jnp.zeros((n_pr, GDN_HEADS, GDN_DK, GDN_DV), F32),
                       jnp.zeros((n_pr, CONV_W - 1, GDN_CONV_CH), F32),
                       jnp.zeros((n_pr, 4, 128, 128), F32),
                       jnp.zeros((n_pr, CONV_W - 1, SSM_CONV_CH), F32))
        xp, st = _layer(xp, mod_p, lw, zero_states, n_pr, seq, fox_p)
        new_p.append(st[:5] + (_ssm_state_unpack(st[5]), st[6]))

        def fox_s(proj, p3, logf, l=l):
            qkv = [p3[:, :, C_BQKV + i * MIX_W:C_BQKV + (i + 1) * MIX_W] for i in range(3)]
            return _fox_sample(*qkv, logf, cache_kt, cache_vt, cache_lf, page_table, l).astype(BF16)

        s_states = (state_gdn[l], state_gdn_conv[l], _ssm_state_pack(state_ssm[l]), state_ssm_conv[l])
        xs, st = _layer(xs, mod_s, lw, s_states, n_dec, dec_seq, fox_s)
        new_s.append(st[:5] + (_ssm_state_unpack(st[5]), st[6]))
    y_prompt = _final_norm(xp, final_norm).reshape(n_pr, seq, d)
    y_sample = _final_norm(xs, final_norm).reshape(n_dec, dec_seq, d)
    k_p, v_p, f_p, g_p, gc_p, s_p, sc_p = [jnp.stack(z) for z in zip(*new_p)]
    k_s, v_s, f_s, g_s, gc_s, s_s, sc_s = [jnp.stack(z) for z in zip(*new_s)]
    return (y_prompt, y_sample, k_p, v_p, f_p, g_p, gc_p, s_p, sc_p,
            k_s, v_s, f_s, g_s, gc_s, s_s, sc_s)
```
